```python
import math
import jax, jax.numpy as jnp
from jax import lax
import numpy as np

D_MODEL = 4096
BATCH = 4
SEQ = 2048
DEPTH = 4
DEC_BATCH = 128
DEC_SEQ = 8
PAST_LEN = 8192
PAGE_SIZE = 128

POOL_WIDTH = D_MODEL // 4
POOL_WINDOWS = (2, 4, 8, 16)
POOL_GROUP = POOL_WIDTH // len(POOL_WINDOWS)
POOL_HIST = max(POOL_WINDOWS) - 1
SSD_WIDTH = 3 * D_MODEL // 8
SSD_HEAD_DIM = 64
SSD_HEADS = SSD_WIDTH // SSD_HEAD_DIM
SSD_GROUPS = 4
SSD_STATE = 128
SSD_CONV = 4
SSD_CHUNK = 128
SSD_CONV_DIM = SSD_WIDTH + 2 * SSD_GROUPS * SSD_STATE
MLA_WIDTH = D_MODEL - POOL_WIDTH - SSD_WIDTH
MLA_V_DIM = 128
MLA_HEADS = MLA_WIDTH // MLA_V_DIM
MLA_NOPE = 128
MLA_ROPE = 64
MLA_Q_RANK = 3 * D_MODEL // 16
MLA_KV_RANK = D_MODEL // 16
ROPE_THETA = 10000.0
Q_BLOCK = 128
IN_SPLITS = (POOL_WIDTH, SSD_WIDTH, SSD_CONV_DIM, SSD_HEADS, MLA_Q_RANK, MLA_KV_RANK, MLA_ROPE)
D_IN = POOL_WIDTH + SSD_WIDTH + SSD_CONV_DIM + SSD_HEADS + MLA_Q_RANK + MLA_KV_RANK + MLA_ROPE
PEER_HEADS = 8
PEER_NKEYS = 128
PEER_EXPERTS = PEER_NKEYS * PEER_NKEYS
PEER_TOPK = 16
PEER_DKEY = 256
PEER_BLOCK = 128
N_MOD = 6
EPS = 1e-6

kernel_name = 'hybrid_pool_ssd_mla_peer_adaln_step'

F32 = jnp.float32


def rmsnorm(x, g):
    xf = x.astype(F32)
    y = xf * lax.rsqrt(jnp.mean(xf * xf, axis=-1, keepdims=True) + EPS)
    return (y * g.astype(F32)).astype(x.dtype)


def rope(x, pos):
    half = MLA_ROPE // 2
    inv_freq = ROPE_THETA ** (-jnp.arange(half, dtype=F32) / half)
    ang = pos.astype(F32)[:, None] * inv_freq[None, :]
    bshape = (1, pos.shape[0]) + (1,) * (x.ndim - 3) + (half,)
    cos, sin = jnp.cos(ang).reshape(bshape), jnp.sin(ang).reshape(bshape)
    xf = x.astype(F32)
    x1, x2 = xf[..., :half], xf[..., half:]
    return jnp.concatenate([x1 * cos - x2 * sin, x1 * sin + x2 * cos], axis=-1).astype(x.dtype)


def adaln_base(c, w_ada, b_ada):
    return (jax.nn.silu(c) @ w_ada + b_ada).reshape(c.shape[0], N_MOD, D_MODEL)


def pool_mixer(u, hist, start, w_pool, pool_scale):
    b, l, _ = u.shape
    ext = jnp.concatenate([hist, u], axis=1)
    ef = ext.astype(F32)
    cs = jnp.concatenate([jnp.zeros_like(ef[:, :1]), jnp.cumsum(ef, axis=1)], axis=1)
    pos = start + jnp.arange(l)
    hi = cs[:, POOL_HIST + 1:]
    means = []
    for g, w in enumerate(POOL_WINDOWS):
        cols = slice(g * POOL_GROUP, (g + 1) * POOL_GROUP)
        lo = cs[:, POOL_HIST + 1 - w: POOL_HIST + 1 - w + l, cols]
        cnt = jnp.minimum(pos + 1, w).astype(F32)[None, :, None]
        means.append((hi[..., cols] - lo) / cnt)
    d = jnp.concatenate(means, axis=-1) - ef[:, POOL_HIST:]
    d = d.reshape(b, l, len(POOL_WINDOWS), POOL_GROUP)
    y = jnp.einsum('blgc,gcd->blgd', d, w_pool.astype(F32)).reshape(b, l, POOL_WIDTH)
    y = y * pool_scale.astype(F32)
    return y.astype(u.dtype), ext[:, -POOL_HIST:]


def causal_dwconv(xbc, conv_hist, w, bias):
    ext = jnp.concatenate([conv_hist, xbc], axis=1)
    out = lax.conv_general_dilated(ext, w[:, None, :], window_strides=(1,), padding='VALID',
                                   dimension_numbers=('NWC', 'WIO', 'NWC'),
                                   feature_group_count=xbc.shape[-1])
    return out + bias, ext[:, -(SSD_CONV - 1):]


def segsum_exp(a):
    t = a.shape[-1]
    cs = jnp.cumsum(a, axis=-1)
    diff = cs[..., :, None] - cs[..., None, :]
    mask = jnp.tril(jnp.ones((t, t), dtype=bool))
    return jnp.exp(jnp.where(mask, diff, -jnp.inf))


def ssd_scan(x, dt, A, Bm, Cm, init):
    b, l, h, p = x.shape
    g, n = SSD_GROUPS, SSD_STATE
    r = h // g
    q = SSD_CHUNK if l % SSD_CHUNK == 0 else l
    c = l // q
    xd = (x * dt[..., None]).reshape(b, c, q, g, r, p)
    a = jnp.transpose((dt * A).reshape(b, c, q, g, r), (0, 1, 3, 4, 2))
    Bc = Bm.reshape(b, c, q, g, n)
    Cc = Cm.reshape(b, c, q, g, n)
    a_cs = jnp.cumsum(a, axis=-1)
    cb = jnp.einsum('bclgn,bcsgn->bcgls', Cc, Bc)
    wts = cb[:, :, :, None] * segsum_exp(a)
    y_diag = jnp.einsum('bcgrls,bcsgrp->bclgrp', wts, xd)
    decay_states = jnp.transpose(jnp.exp(a_cs[..., -1:] - a_cs), (0, 1, 4, 2, 3))
    states = jnp.einsum('bcsgn,bcsgrp->bcgrpn', Bc, xd * decay_states[..., None])
    def step(s, inp):
        st, da = inp
        return s * jnp.exp(da)[..., None, None] + st, s
    final, states_in = lax.scan(step, init.reshape(b, g, r, p, n),
                                (jnp.swapaxes(states, 0, 1), jnp.swapaxes(a_cs[..., -1], 0, 1)))
    states_in = jnp.swapaxes(states_in, 0, 1)
    decay_out = jnp.transpose(jnp.exp(a_cs), (0, 1, 4, 2, 3))
    y_off = jnp.einsum('bclgn,bcgrpn->bclgrp', Cc, states_in) * decay_out[..., None]
    y = (y_diag + y_off).reshape(b, l, h, p)
    return y, final.reshape(b, h, p, n)


def ssd_mixer(z, xbc, dt_raw, conv_hist, ssm_state, conv_w, conv_b, dt_bias, a_log, d_skip, norm_g):
    b, l, _ = z.shape
    gn = SSD_GROUPS * SSD_STATE
    xbc, new_conv = causal_dwconv(xbc, conv_hist, conv_w, conv_b)
    xbc = jax.nn.silu(xbc.astype(F32))
    xs = xbc[..., :SSD_WIDTH].reshape(b, l, SSD_HEADS, SSD_HEAD_DIM)
    Bm = xbc[..., SSD_WIDTH:SSD_WIDTH + gn].reshape(b, l, SSD_GROUPS, SSD_STATE)
    Cm = xbc[..., SSD_WIDTH + gn:].reshape(b, l, SSD_GROUPS, SSD_STATE)
    dt = jax.nn.softplus(dt_raw.astype(F32) + dt_bias.astype(F32))
    A = -jnp.exp(a_log.astype(F32))
    y, new_state = ssd_scan(xs, dt, A, Bm, Cm, ssm_state.astype(F32))
    y = y + d_skip.astype(F32)[:, None] * xs
    y = (y.reshape(b, l, SSD_WIDTH) * jax.nn.silu(z.astype(F32))).reshape(b, l, SSD_GROUPS, SSD_WIDTH // SSD_GROUPS)
    y = y * lax.rsqrt(jnp.mean(y * y, axis=-1, keepdims=True) + EPS)
    y = y.reshape(b, l, SSD_WIDTH) * norm_g.astype(F32)
    return y.astype(z.dtype), new_conv, new_state.astype(ssm_state.dtype)


def mla_attend(q_lat, q_rope, k_lat, k_rope, q_pos, k_pos):
    b, l, h, c = q_lat.shape
    qb = Q_BLOCK if l % Q_BLOCK == 0 else l
    nb = l // qb
    scale = (MLA_NOPE + MLA_ROPE) ** -0.5

    def block(args):
        ql, qr, qp = args
        s = (jnp.einsum('bqhc,bkc->bhqk', ql, k_lat) + jnp.einsum('bqhr,bkr->bhqk', qr, k_rope)).astype(F32) * scale
        s = jnp.where(k_pos[None, None, None, :] <= qp[None, None, :, None], s, -jnp.inf)
        pr = jax.nn.softmax(s, axis=-1).astype(k_lat.dtype)
        return jnp.einsum('bhqk,bkc->bqhc', pr, k_lat)

    ql = jnp.swapaxes(q_lat.reshape(b, nb, qb, h, c), 0, 1)
    qr = jnp.swapaxes(q_rope.reshape(b, nb, qb, h, MLA_ROPE), 0, 1)
    out = lax.map(block, (ql, qr, q_pos.reshape(nb, qb)))
    return jnp.swapaxes(out, 0, 1).reshape(b, l, h, c)


def mla_mixer(cq, ckv, kr, past_lat, past_rope, start, g_q, w_uq, g_kv, w_uk, w_uv):
    b, l, _ = cq.shape
    past = past_lat.shape[1]
    q_pos = start + jnp.arange(l)
    q = jnp.einsum('blr,rhd->blhd', rmsnorm(cq, g_q), w_uq)
    q_nope, q_rope = q[..., :MLA_NOPE], rope(q[..., MLA_NOPE:], q_pos)
    lat = rmsnorm(ckv, g_kv)
    k_rope = rope(kr, q_pos)
    q_lat = jnp.einsum('blhd,chd->blhc', q_nope, w_uk)
    k_lat_all = jnp.concatenate([past_lat, lat], axis=1)
    k_rope_all = jnp.concatenate([past_rope, k_rope], axis=1)
    k_pos = start - past + jnp.arange(past + l)
    o_lat = mla_attend(q_lat, q_rope, k_lat_all, k_rope_all, q_pos, k_pos)
    o = jnp.einsum('blhc,chv->blhv', o_lat, w_uv).reshape(b, l, MLA_HEADS * MLA_V_DIM)
    return o, lat, k_rope


def peer_ffn(h, wq, sub_keys, u_tab, v_tab):
    b, l, d = h.shape
    n = b * l
    pad = (-n) % PEER_BLOCK
    tb = jnp.pad(h.reshape(n, d), ((0, pad), (0, 0))).reshape(-1, PEER_BLOCK, d)
    kk = PEER_TOPK

    def block(xb):
        t = xb.shape[0]
        q = (xb @ wq).reshape(t, PEER_HEADS, 2, PEER_DKEY // 2)
        s = jnp.einsum('thsk,hsnk->thsn', q, sub_keys).astype(F32)
        sv, si = lax.top_k(s, kk)
        cand = (sv[:, :, 0, :, None] + sv[:, :, 1, None, :]).reshape(t, PEER_HEADS, kk * kk)
        cidx = (si[:, :, 0, :, None] * PEER_NKEYS + si[:, :, 1, None, :]).reshape(t, PEER_HEADS, kk * kk)
        best, sel = lax.top_k(cand, kk)
        eidx = jnp.take_along_axis(cidx, sel, axis=-1).reshape(t, PEER_HEADS * kk)
        gate = jax.nn.softmax(best, axis=-1).reshape(t, PEER_HEADS * kk)
        act = jax.nn.gelu(jnp.einsum('tkd,td->tk', u_tab[eidx], xb).astype(F32), approximate=False)
        coef = (gate * act).astype(xb.dtype)
        return jnp.einsum('tk,tkd->td', coef, v_tab[eidx])

    out = lax.map(block, tb).reshape(-1, d)[:n]
    return out.reshape(b, l, d)


def decoder_layer(x, mod, pool_hist, conv_hist, ssm_state, past_lat, past_rope, start, p):
    shift1, scale1, gate1, shift2, scale2, gate2 = (mod[:, i, None, :] for i in range(N_MOD))
    hn = rmsnorm(x, p['g_mix']) * (1 + scale1) + shift1
    u = hn @ p['w_in']
    u_pool, z, xbc, dt_raw, cq, ckv, kr = jnp.split(u, [int(o) for o in np.cumsum(IN_SPLITS)[:-1]], axis=-1)
    y_pool, new_pool = pool_mixer(u_pool, pool_hist, start, p['w_pool'], p['pool_scale'])
    y_ssd, new_conv, new_ssm = ssd_mixer(z, xbc, dt_raw, conv_hist, ssm_state, p['conv_w'], p['conv_b'],
                                         p['dt_bias'], p['a_log'], p['d_skip'], p['ssd_norm'])
    y_mla, lat, krope = mla_mixer(cq, ckv, kr, past_lat, past_rope, start,
                                  p['g_q'], p['w_uq'], p['g_kv'], p['w_uk'], p['w_uv'])
    mixed = jnp.concatenate([y_pool, y_ssd, y_mla], axis=-1) @ p['w_out']
    x = x + gate1 * mixed
    hn = rmsnorm(x, p['g_ffn']) * (1 + scale2) + shift2
    x = x + gate2 * peer_ffn(hn, p['peer_wq'], p['peer_keys'], p['peer_u'], p['peer_v'])
    return x, new_pool, new_conv, new_ssm, lat, krope


def setup_inputs(seed: int = 0) -> dict:
    key = jax.random.key(seed)
    k = jax.random.split(key, 40)
    nrm = lambda kk, shape, s: jax.random.normal(kk, shape, F32) * s
    n_pages = PAST_LEN // PAGE_SIZE
    n_used = DEC_BATCH * n_pages
    n_phys = n_used + n_used // 4
    page_table = jax.random.permutation(k[0], n_phys)[:n_used].reshape(DEC_BATCH, n_pages).astype(jnp.int32)
    dt0 = jnp.exp(jax.random.uniform(k[1], (DEPTH, SSD_HEADS), F32, math.log(1e-3), math.log(1e-1)))
    dt_bias = dt0 + jnp.log(-jnp.expm1(-dt0))
    a_log = jnp.log(jax.random.uniform(k[2], (DEPTH, SSD_HEADS), F32, 1.0, 16.0))
    return {
        'x_prompt': nrm(k[3], (BATCH, SEQ, D_MODEL), 1.0),
        'x_sample': nrm(k[4], (DEC_BATCH, DEC_SEQ, D_MODEL), 1.0),
        'cache_latent': nrm(k[5], (DEPTH, n_phys, PAGE_SIZE, MLA_KV_RANK), 1.0),
        'cache_rope': nrm(k[6], (DEPTH, n_phys, PAGE_SIZE, MLA_ROPE), 1.0),
        'state_ssm': nrm(k[7], (DEPTH, DEC_BATCH, SSD_HEADS, SSD_HEAD_DIM, SSD_STATE), 0.1),
        'state_conv': nrm(k[8], (DEPTH, DEC_BATCH, SSD_CONV - 1, SSD_CONV_DIM), 1.0),
        'state_pool': nrm(k[9], (DEPTH, DEC_BATCH, POOL_HIST, POOL_WIDTH), 1.0),
        'page_table': page_table,
        'c_prompt': nrm(k[10], (BATCH, D_MODEL), 1.0),
        'c_sample': nrm(k[11], (DEC_BATCH, D_MODEL), 1.0),
        'w_ada': nrm(k[12], (D_MODEL, N_MOD * D_MODEL), 0.2 * D_MODEL ** -0.5),
        'b_ada': nrm(k[13], (N_MOD * D_MODEL,), 0.01),
        'ada_table': nrm(k[14], (DEPTH, N_MOD, D_MODEL), 0.2),
        'g_mix': 1.0 + nrm(k[15], (DEPTH, D_MODEL), 0.01),
        'w_in': nrm(k[16], (DEPTH, D_MODEL, D_IN), D_MODEL ** -0.5),
        'w_pool': nrm(k[17], (DEPTH, len(POOL_WINDOWS), POOL_GROUP, POOL_GROUP), POOL_GROUP ** -0.5),
        'pool_scale': 1.0 + nrm(k[18], (DEPTH, POOL_WIDTH), 0.1),
        'conv_w': nrm(k[19], (DEPTH, SSD_CONV, SSD_CONV_DIM), SSD_CONV ** -0.5),
        'conv_b': nrm(k[20], (DEPTH, SSD_CONV_DIM), 0.01),
        'dt_bias': dt_bias,
        'a_log': a_log,
        'd_skip': 1.0 + nrm(k[21], (DEPTH, SSD_HEADS), 0.1),
        'ssd_norm': 1.0 + nrm(k[22], (DEPTH, SSD_WIDTH), 0.01),
        'g_q': 1.0 + nrm(k[23], (DEPTH, MLA_Q_RANK), 0.01),
        'w_uq': nrm(k[24], (DEPTH, MLA_Q_RANK, MLA_HEADS, MLA_NOPE + MLA_ROPE), MLA_Q_RANK ** -0.5),
        'g_kv': 1.0 + nrm(k[25], (DEPTH, MLA_KV_RANK), 0.01),
        'w_uk': nrm(k[26], (DEPTH, MLA_KV_RANK, MLA_HEADS, MLA_NOPE), MLA_KV_RANK ** -0.5),
        'w_uv': nrm(k[27], (DEPTH, MLA_KV_RANK, MLA_HEADS, MLA_V_DIM), MLA_KV_RANK ** -0.5),
        'w_out': nrm(k[28], (DEPTH, D_MODEL, D_MODEL), D_MODEL ** -0.5),
        'g_ffn': 1.0 + nrm(k[29], (DEPTH, D_MODEL), 0.01),
        'peer_wq': nrm(k[30], (DEPTH, D_MODEL, PEER_HEADS * PEER_DKEY), D_MODEL ** -0.5),
        'peer_keys': nrm(k[31], (DEPTH, PEER_HEADS, 2, PEER_NKEYS, PEER_DKEY // 2), (PEER_DKEY // 2) ** -0.5),
        'peer_u': nrm(k[32], (DEPTH, PEER_EXPERTS, D_MODEL), D_MODEL ** -0.5),
        'peer_v': nrm(k[33], (DEPTH, PEER_EXPERTS, D_MODEL), (PEER_HEADS * PEER_TOPK) ** -0.5),
        'g_final': 1.0 + nrm(k[34], (D_MODEL,), 0.01),
    }


def reference(x_prompt, x_sample, cache_latent, cache_rope, state_ssm, state_conv, state_pool, page_table,
              c_prompt, c_sample, w_ada, b_ada, ada_table, g_mix, w_in, w_pool, pool_scale, conv_w, conv_b,
              dt_bias, a_log, d_skip, ssd_norm, g_q, w_uq, g_kv, w_uk, w_uv, w_out, g_ffn,
              peer_wq, peer_keys, peer_u, peer_v, g_final):
    bp = x_prompt.shape[0]
    bs = x_sample.shape[0]
    past_len = page_table.shape[1] * PAGE_SIZE
    dtype = x_prompt.dtype
    mod_p = adaln_base(c_prompt, w_ada, b_ada)
    mod_s = adaln_base(c_sample, w_ada, b_ada)
    zero_pool = jnp.zeros((bp, POOL_HIST, POOL_WIDTH), dtype)
    zero_conv = jnp.zeros((bp, SSD_CONV - 1, SSD_CONV_DIM), dtype)
    zero_ssm = jnp.zeros((bp, SSD_HEADS, SSD_HEAD_DIM, SSD_STATE), dtype)
    no_lat = jnp.zeros((bp, 0, MLA_KV_RANK), dtype)
    no_rope = jnp.zeros((bp, 0, MLA_ROPE), dtype)
    hp, hs = x_prompt, x_sample
    lat_p, rope_p, lat_s, rope_s = [], [], [], []
    ssm_p, ssm_s, conv_p, conv_s, pool_p, pool_s = [], [], [], [], [], []
    for l in range(DEPTH):
        p = {'g_mix': g_mix[l], 'w_in': w_in[l], 'w_pool': w_pool[l], 'pool_scale': pool_scale[l],
             'conv_w': conv_w[l], 'conv_b': conv_b[l], 'dt_bias': dt_bias[l], 'a_log': a_log[l],
             'd_skip': d_skip[l], 'ssd_norm': ssd_norm[l], 'g_q': g_q[l], 'w_uq': w_uq[l],
             'g_kv': g_kv[l], 'w_uk': w_uk[l], 'w_uv': w_uv[l], 'w_out': w_out[l], 'g_ffn': g_ffn[l],
             'peer_wq': peer_wq[l], 'peer_keys': peer_keys[l], 'peer_u': peer_u[l], 'peer_v': peer_v[l]}
        hp, np_pool, np_conv, np_ssm, np_lat, np_rope = decoder_layer(
            hp, mod_p + ada_table[l], zero_pool, zero_conv, zero_ssm, no_lat, no_rope, 0, p)
        past_lat = cache_latent[l, page_table].reshape(bs, past_len, MLA_KV_RANK)
        past_rope = cache_rope[l, page_table].reshape(bs, past_len, MLA_ROPE)
        hs, ns_pool, ns_conv, ns_ssm, ns_lat, ns_rope = decoder_layer(
            hs, mod_s + ada_table[l], state_pool[l], state_conv[l], state_ssm[l], past_lat, past_rope, past_len, p)
        lat_p.append(np_lat); rope_p.append(np_rope); lat_s.append(ns_lat); rope_s.append(ns_rope)
        ssm_p.append(np_ssm); ssm_s.append(ns_ssm); conv_p.append(np_conv); conv_s.append(ns_conv)
        pool_p.append(np_pool); pool_s.append(ns_pool)
    y_prompt = rmsnorm(hp, g_final)
    y_sample = rmsnorm(hs, g_final)
    return (y_prompt, y_sample, jnp.stack(lat_p), jnp.stack(rope_p), jnp.stack(lat_s), jnp.stack(rope_s),
            jnp.stack(ssm_p), jnp.stack(ssm_s), jnp.stack(conv_p), jnp.stack(conv_s),
            jnp.stack(pool_p), jnp.stack(pool_s))
```

```python
import functools
import math

import jax
import jax.numpy as jnp
import numpy as np
from jax import lax
from jax.experimental import pallas as pl
from jax.experimental.pallas import tpu as pltpu

F32 = jnp.float32
BF16 = jnp.bfloat16

D_MODEL = 4096
N_MOD = 6
EPS = 1e-6
PAGE_SIZE = 128

POOL_WIDTH = D_MODEL // 4
POOL_WINDOWS = (2, 4, 8, 16)
POOL_GROUP = POOL_WIDTH // len(POOL_WINDOWS)
POOL_HIST = max(POOL_WINDOWS) - 1

SSD_WIDTH = 3 * D_MODEL // 8
SSD_HEAD_DIM = 64
SSD_HEADS = SSD_WIDTH // SSD_HEAD_DIM
SSD_GROUPS = 4
SSD_STATE = 128
SSD_CONV = 4
SSD_CHUNK = 128
SSD_CONV_DIM = SSD_WIDTH + 2 * SSD_GROUPS * SSD_STATE

MLA_WIDTH = D_MODEL - POOL_WIDTH - SSD_WIDTH
MLA_V_DIM = 128
MLA_HEADS = MLA_WIDTH // MLA_V_DIM
MLA_NOPE = 128
MLA_ROPE = 64
MLA_Q_RANK = 3 * D_MODEL // 16
MLA_KV_RANK = D_MODEL // 16
ROPE_THETA = 10000.0
Q_BLOCK = 128

PEER_HEADS = 8
PEER_NKEYS = 128
PEER_EXPERTS = PEER_NKEYS * PEER_NKEYS
PEER_TOPK = 16
PEER_DKEY = 256

OFF_POOL = 0
OFF_Z = OFF_POOL + POOL_WIDTH
OFF_XBC = OFF_Z + SSD_WIDTH
OFF_CQ = OFF_XBC + SSD_CONV_DIM
OFF_CKV = OFF_CQ + MLA_Q_RANK
OFF_KR = OFF_CKV + MLA_KV_RANK
OFF_DT = OFF_KR + MLA_ROPE
D_IN_PACKED = OFF_DT + 64
D_IN_RAW_DT = POOL_WIDTH + SSD_WIDTH + SSD_CONV_DIM

VMEM_LIMIT = 56 * 1024 * 1024
ROW_TILE = 512


def _cparams(n_axes):
    return pltpu.CompilerParams(dimension_semantics=("arbitrary",) * n_axes,
                                vmem_limit_bytes=VMEM_LIMIT)


def _token_tiling(x3):
    g, r, _ = x3.shape
    if r >= ROW_TILE:
        assert r % ROW_TILE == 0
        return 1, ROW_TILE
    tg = min(ROW_TILE // r, 32)
    assert ROW_TILE % r == 0 and g % tg == 0
    return tg, r


def _adaln_kernel(c_ref, w_ref, b_ref, o_ref):
    c = c_ref[...]
    a = (c * jax.nn.sigmoid(c)).astype(BF16)
    o_ref[...] = jnp.dot(a, w_ref[...].astype(BF16), preferred_element_type=F32) + b_ref[...]


def adaln_base(c, w_ada, b_ada):
    n, d = c.shape
    dout = w_ada.shape[1]
    tn = 512
    return pl.pallas_call(
        _adaln_kernel,
        grid=(dout // tn,),
        in_specs=[pl.BlockSpec((n, d), lambda j: (0, 0)),
                  pl.BlockSpec((d, tn), lambda j: (0, j)),
                  pl.BlockSpec((1, tn), lambda j: (0, j))],
        out_specs=pl.BlockSpec((n, tn), lambda j: (0, j)),
        out_shape=jax.ShapeDtypeStruct((n, dout), F32),
        compiler_params=_cparams(1),
        name="adaln",
    )(c, w_ada, b_ada.reshape(1, dout))


def _norm_mod_matmul_kernel(x_ref, g_ref, sc_ref, sh_ref, w_ref, o_ref, *rest, emit_hn):
    if emit_hn:
        hn_out_ref, hn_s = rest
    else:
        (hn_s,) = rest

    @pl.when(pl.program_id(2) == 0)
    def _():
        x = x_ref[...]
        y = x * lax.rsqrt(jnp.mean(x * x, axis=-1, keepdims=True) + EPS) * g_ref[...]
        hn = (y * sc_ref[...] + sh_ref[...]).reshape(hn_s.shape).astype(BF16)
        hn_s[...] = hn
        if emit_hn:
            hn_out_ref[...] = hn

    o_ref[...] = jnp.dot(hn_s[...], w_ref[...], preferred_element_type=F32)


def norm_mod_matmul(x3, g, scale1p, shift, w, tn, emit_hn=False, name="norm_mod_matmul"):
    gdim, r, d = x3.shape
    dout = w.shape[1]
    tg, rb = _token_tiling(x3)
    tm = tg * rb
    n_r = r // rb
    grid = (gdim // tg, n_r, dout // tn)
    out_shape = [jax.ShapeDtypeStruct((gdim * r, dout), F32)]
    out_specs = [pl.BlockSpec((tm, tn), lambda gi, ri, j: (gi * n_r + ri, j))]
    if emit_hn:
        out_shape.append(jax.ShapeDtypeStruct((gdim * r, d), BF16))
        out_specs.append(pl.BlockSpec((tm, d), lambda gi, ri, j: (gi * n_r + ri, 0)))
    res = pl.pallas_call(
        functools.partial(_norm_mod_matmul_kernel, emit_hn=emit_hn),
        grid=grid,
        in_specs=[pl.BlockSpec((tg, rb, d), lambda gi, ri, j: (gi, ri, 0)),
                  pl.BlockSpec((1, 1, d), lambda gi, ri, j: (0, 0, 0)),
                  pl.BlockSpec((tg, 1, d), lambda gi, ri, j: (gi, 0, 0)),
                  pl.BlockSpec((tg, 1, d), lambda gi, ri, j: (gi, 0, 0)),
                  pl.BlockSpec((d, tn), lambda gi, ri, j: (0, j))],
        out_specs=out_specs,
        out_shape=out_shape,
        scratch_shapes=[pltpu.VMEM((tm, d), BF16)],
        compiler_params=_cparams(3),
        name=name,
    )(x3, g.reshape(1, 1, d), scale1p, shift, w)
    return res if emit_hn else res[0]


def _proj_residual_kernel(yp_ref, ys_ref, ym_ref, w1_ref, w2_ref, w3_ref, x_ref, gate_ref, o_ref):
    acc = jnp.dot(yp_ref[...], w1_ref[...], preferred_element_type=F32)
    acc += jnp.dot(ys_ref[...], w2_ref[...], preferred_element_type=F32)
    acc += jnp.dot(ym_ref[...], w3_ref[...], preferred_element_type=F32)
    o_ref[...] = x_ref[...] + gate_ref[...] * acc.reshape(x_ref.shape)


def proj_residual(y_pool, y_ssd, y_mla, w1, w2, w3, x3, gate):
    gdim, r, d = x3.shape
    tg, rb = _token_tiling(x3)
    tm = tg * rb
    n_r = r // rb
    tn = 512
    row = lambda gi, ri, j: (gi * n_r + ri, 0)
    return pl.pallas_call(
        _proj_residual_kernel,
        grid=(gdim // tg, n_r, d // tn),
        in_specs=[pl.BlockSpec((tm, y_pool.shape[1]), row),
                  pl.BlockSpec((tm, y_ssd.shape[1]), row),
                  pl.BlockSpec((tm, y_mla.shape[1]), row),
                  pl.BlockSpec((w1.shape[0], tn), lambda gi, ri, j: (0, j)),
                  pl.BlockSpec((w2.shape[0], tn), lambda gi, ri, j: (0, j)),
                  pl.BlockSpec((w3.shape[0], tn), lambda gi, ri, j: (0, j)),
                  pl.BlockSpec((tg, rb, tn), lambda gi, ri, j: (gi, ri, j)),
                  pl.BlockSpec((tg, 1, tn), lambda gi, ri, j: (gi, 0, j))],
        out_specs=pl.BlockSpec((tg, rb, tn), lambda gi, ri, j: (gi, ri, j)),
        out_shape=jax.ShapeDtypeStruct(x3.shape, F32),
        compiler_params=_cparams(3),
        name="proj_residual",
    )(y_pool, y_ssd, y_mla, w1, w2, w3, x3, gate)


ROUTER_TILE = 256


def _top16_rows(s):
    iota = lax.broadcasted_iota(jnp.int32, s.shape, 0)
    rem = s
    vals = []
    for _ in range(PEER_TOPK):
        m = jnp.max(rem, axis=0, keepdims=True)
        idx = jnp.min(jnp.where(rem == m, iota, PEER_NKEYS), axis=0, keepdims=True)
        rem = jnp.where(iota == idx, -jnp.inf, rem)
        vals.append(m)
    return rem != s, jnp.concatenate(vals, axis=0)


def _router_kernel(q_ref, keys_ref, s0_ref, s1_ref, e0_ref, e1_ref, thr_ref):
    half = PEER_DKEY // 2

    def per_head(h, carry):
        def scores(side):
            col = pl.multiple_of(h * PEER_DKEY + side * half, half)
            qh = q_ref[:, pl.ds(col, half)].astype(BF16)
            return lax.dot_general(keys_ref[h, side], qh, (((1,), (1,)), ((), ())),
                                   preferred_element_type=F32)

        s0, s1 = scores(0), scores(1)
        mask0, sv0 = _top16_rows(s0)
        mask1, sv1 = _top16_rows(s1)
        cand = jnp.concatenate([sv0[a:a + 1] + sv1 for a in range(PEER_TOPK)], axis=0)
        cmax = sv0[0:1] + sv1[0:1]
        rem = cand
        cum = jnp.zeros_like(cmax)
        thr = cmax
        for _ in range(PEER_TOPK):
            m = jnp.max(rem, axis=0, keepdims=True)
            eq = rem == m
            thr = jnp.where(cum < PEER_TOPK, m, thr)
            cum = cum + jnp.sum(eq.astype(F32), axis=0, keepdims=True)
            rem = jnp.where(eq, -jnp.inf, rem)
        z = jnp.sum(jnp.where(cand >= thr, jnp.exp(cand - cmax), 0.0), axis=0, keepdims=True)
        s0_ref[h] = s0
        s1_ref[h] = s1
        e0_ref[h] = jnp.where(mask0, jnp.exp(s0 - sv0[0:1]), 0.0)
        e1_ref[h] = jnp.where(mask1, jnp.exp(s1 - sv1[0:1]), 0.0) / z
        thr_ref[pl.ds(h, 1), :] = thr
        return carry

    lax.fori_loop(0, PEER_HEADS, per_head, 0)


def peer_router(q, keys_bf16):
    n = q.shape[0]
    t = ROUTER_TILE
    big = jax.ShapeDtypeStruct((PEER_HEADS, PEER_NKEYS, n), F32)
    big_spec = pl.BlockSpec((PEER_HEADS, PEER_NKEYS, t), lambda i: (0, 0, i))
    return pl.pallas_call(
        _router_kernel,
        grid=(n // t,),
        in_specs=[pl.BlockSpec((t, PEER_HEADS * PEER_DKEY), lambda i: (i, 0)),
                  pl.BlockSpec(keys_bf16.shape, lambda i: (0, 0, 0, 0))],
        out_specs=[big_spec, big_spec, big_spec, big_spec,
                   pl.BlockSpec((PEER_HEADS, t), lambda i: (0, i))],
        out_shape=[big, big, big, big, jax.ShapeDtypeStruct((PEER_HEADS, n), F32)],
        compiler_params=_cparams(1),
        name="peer_router",
    )(q, keys_bf16)


PEER_TOKEN_TILE = 512
PEER_EXPERT_TILE = 512


def _gelu_exact(a):
    return 0.5 * a * (1.0 + lax.erf(a * (2.0 ** -0.5)))


def _peer_dense_kernel(hn_ref, u_ref, v_ref, s0_ref, s1_ref, e0_ref, e1_ref, thr_ref, o_ref):
    e = pl.program_id(1)
    rows_per_tile = PEER_EXPERT_TILE // PEER_NKEYS

    @pl.when(e == 0)
    def _():
        o_ref[...] = jnp.zeros_like(o_ref)

    act_t = lax.dot_general(u_ref[...], hn_ref[...], (((1,), (1,)), ((), ())),
                            preferred_element_type=F32)
    coef = []
    for ib in range(rows_per_tile):
        i = e * rows_per_tile + ib
        gate = jnp.zeros((PEER_NKEYS, act_t.shape[1]), F32)
        for h in range(PEER_HEADS):
            picked = (s0_ref[h, pl.ds(i, 1), :] + s1_ref[h]) >= thr_ref[pl.ds(h, 1), :]
            gate = gate + jnp.where(picked, e0_ref[h, pl.ds(i, 1), :] * e1_ref[h], 0.0)
        a = act_t[ib * PEER_NKEYS:(ib + 1) * PEER_NKEYS]
        coef.append((gate * _gelu_exact(a)).astype(BF16))
    coef_t = jnp.concatenate(coef, axis=0)
    o_ref[...] += lax.dot_general(coef_t, v_ref[...], (((0,), (0,)), ((), ())),
                                  preferred_element_type=F32)


def peer_dense(hn, u_bf16, v_bf16, s0, s1, e0, e1, thr):
    n, d = hn.shape
    tm, te = PEER_TOKEN_TILE, PEER_EXPERT_TILE
    once = pl.Buffered(1)
    big_spec = pl.BlockSpec((PEER_HEADS, PEER_NKEYS, tm), lambda i, e: (0, 0, i), pipeline_mode=once)
    return pl.pallas_call(
        _peer_dense_kernel,
        grid=(n // tm, PEER_EXPERTS // te),
        in_specs=[pl.BlockSpec((tm, d), lambda i, e: (i, 0), pipeline_mode=once),
                  pl.BlockSpec((te, d), lambda i, e: (e, 0)),
                  pl.BlockSpec((te, d), lambda i, e: (e, 0)),
                  big_spec, big_spec, big_spec, big_spec,
                  pl.BlockSpec((PEER_HEADS, tm), lambda i, e: (0, i), pipeline_mode=once)],
        out_specs=pl.BlockSpec((tm, d), lambda i, e: (i, 0)),
        out_shape=jax.ShapeDtypeStruct((n, d), F32),
        compiler_params=_cparams(2),
        name="peer_dense",
    )(hn, u_bf16, v_bf16, s0, s1, e0, e1, thr)


def _gated_residual_kernel(x_ref, gate_ref, y_ref, o_ref):
    o_ref[...] = x_ref[...] + gate_ref[...] * y_ref[...].reshape(x_ref.shape)


def gated_residual(x3, gate, y):
    gdim, r, d = x3.shape
    tg, rb = _token_tiling(x3)
    n_r = r // rb
    return pl.pallas_call(
        _gated_residual_kernel,
        grid=(gdim // tg, n_r),
        in_specs=[pl.BlockSpec((tg, rb, d), lambda gi, ri: (gi, ri, 0)),
                  pl.BlockSpec((tg, 1, d), lambda gi, ri: (gi, 0, 0)),
                  pl.BlockSpec((tg * rb, d), lambda gi, ri: (gi * n_r + ri, 0))],
        out_specs=pl.BlockSpec((tg, rb, d), lambda gi, ri: (gi, ri, 0)),
        out_shape=jax.ShapeDtypeStruct(x3.shape, F32),
        compiler_params=_cparams(2),
        name="gated_residual",
    )(x3, gate, y)


def _rmsnorm_kernel(x_ref, g_ref, o_ref):
    x = x_ref[...]
    o_ref[...] = x * lax.rsqrt(jnp.mean(x * x, axis=-1, keepdims=True) + EPS) * g_ref[...]


def rmsnorm_rows(x3, g):
    gdim, r, d = x3.shape
    tg, rb = _token_tiling(x3)
    return pl.pallas_call(
        _rmsnorm_kernel,
        grid=(gdim // tg, r // rb),
        in_specs=[pl.BlockSpec((tg, rb, d), lambda gi, ri: (gi, ri, 0)),
                  pl.BlockSpec((1, 1, d), lambda gi, ri: (0, 0, 0))],
        out_specs=pl.BlockSpec((tg, rb, d), lambda gi, ri: (gi, ri, 0)),
        out_shape=jax.ShapeDtypeStruct(x3.shape, F32),
        compiler_params=_cparams(2),
        name="final_rmsnorm",
    )(x3, g.reshape(1, 1, d))


def _rmsnorm(x, g):
    return x * lax.rsqrt(jnp.mean(x * x, axis=-1, keepdims=True) + EPS) * g


def _rope(x, pos):
    half = MLA_ROPE // 2
    inv_freq = ROPE_THETA ** (-jnp.arange(half, dtype=F32) / half)
    ang = pos.astype(F32)[:, None] * inv_freq[None, :]
    bshape = (1, pos.shape[0]) + (1,) * (x.ndim - 3) + (half,)
    cos, sin = jnp.cos(ang).reshape(bshape), jnp.sin(ang).reshape(bshape)
    x1, x2 = x[..., :half], x[..., half:]
    return jnp.concatenate([x1 * cos - x2 * sin, x1 * sin + x2 * cos], axis=-1)


def _pool_mixer(u, hist, start, w_pool, pool_scale):
    b, l, _ = u.shape
    ext = jnp.concatenate([hist, u], axis=1)
    cs = jnp.concatenate([jnp.zeros_like(ext[:, :1]), jnp.cumsum(ext, axis=1)], axis=1)
    pos = start + jnp.arange(l)
    hi = cs[:, POOL_HIST + 1:]
    means = []
    for g, w in enumerate(POOL_WINDOWS):
        cols = slice(g * POOL_GROUP, (g + 1) * POOL_GROUP)
        lo = cs[:, POOL_HIST + 1 - w: POOL_HIST + 1 - w + l, cols]
        cnt = jnp.minimum(pos + 1, w).astype(F32)[None, :, None]
        means.append((hi[..., cols] - lo) / cnt)
    d = jnp.concatenate(means, axis=-1) - ext[:, POOL_HIST:]
    d = d.reshape(b, l, len(POOL_WINDOWS), POOL_GROUP)
    y = jnp.einsum('blgc,gcd->blgd', d, w_pool).reshape(b, l, POOL_WIDTH)
    return y * pool_scale, ext[:, -POOL_HIST:]


def _causal_dwconv(xbc, conv_hist, w, bias):
    ext = jnp.concatenate([conv_hist, xbc], axis=1)
    out = lax.conv_general_dilated(ext, w[:, None, :], window_strides=(1,), padding='VALID',
                                   dimension_numbers=('NWC', 'WIO', 'NWC'),
                                   feature_group_count=xbc.shape[-1])
    return out + bias, ext[:, -(SSD_CONV - 1):]


def _segsum_exp(a):
    t = a.shape[-1]
    cs = jnp.cumsum(a, axis=-1)
    diff = cs[..., :, None] - cs[..., None, :]
    mask = jnp.tril(jnp.ones((t, t), dtype=bool))
    return jnp.exp(jnp.where(mask, diff, -jnp.inf))


def _ssd_scan(x, dt, A, Bm, Cm, init):
    b, l, h, p = x.shape
    g, n = SSD_GROUPS, SSD_STATE
    r = h // g
    q = SSD_CHUNK if l % SSD_CHUNK == 0 else l
    c = l // q
    xd = (x * dt[..., None]).reshape(b, c, q, g, r, p)
    a = jnp.transpose((dt * A).reshape(b, c, q, g, r), (0, 1, 3, 4, 2))
    Bc = Bm.reshape(b, c, q, g, n)
    Cc = Cm.reshape(b, c, q, g, n)
    a_cs = jnp.cumsum(a, axis=-1)
    cb = jnp.einsum('bclgn,bcsgn->bcgls', Cc, Bc)
    wts = cb[:, :, :, None] * _segsum_exp(a)
    y_diag = jnp.einsum('bcgrls,bcsgrp->bclgrp', wts, xd)
    decay_states = jnp.transpose(jnp.exp(a_cs[..., -1:] - a_cs), (0, 1, 4, 2, 3))
    states = jnp.einsum('bcsgn,bcsgrp->bcgrpn', Bc, xd * decay_states[..., None])

    def step(s, inp):
        st, da = inp
        return s * jnp.exp(da)[..., None, None] + st, s

    final, states_in = lax.scan(step, init.reshape(b, g, r, p, n),
                                (jnp.swapaxes(states, 0, 1), jnp.swapaxes(a_cs[..., -1], 0, 1)))
    states_in = jnp.swapaxes(states_in, 0, 1)
    decay_out = jnp.transpose(jnp.exp(a_cs), (0, 1, 4, 2, 3))
    y_off = jnp.einsum('bclgn,bcgrpn->bclgrp', Cc, states_in) * decay_out[..., None]
    y = (y_diag + y_off).reshape(b, l, h, p)
    return y, final.reshape(b, h, p, n)


def _ssd_mixer(z, xbc, dt_raw, conv_hist, ssm_state, conv_w, conv_b, dt_bias, a_log, d_skip, norm_g):
    b, l, _ = z.shape
    gn = SSD_GROUPS * SSD_STATE
    xbc, new_conv = _causal_dwconv(xbc, conv_hist, conv_w, conv_b)
    xbc = jax.nn.silu(xbc)
    xs = xbc[..., :SSD_WIDTH].reshape(b, l, SSD_HEADS, SSD_HEAD_DIM)
    Bm = xbc[..., SSD_WIDTH:SSD_WIDTH + gn].reshape(b, l, SSD_GROUPS, SSD_STATE)
    Cm = xbc[..., SSD_WIDTH + gn:].reshape(b, l, SSD_GROUPS, SSD_STATE)
    dt = jax.nn.softplus(dt_raw + dt_bias)
    A = -jnp.exp(a_log)
    y, new_state = _ssd_scan(xs, dt, A, Bm, Cm, ssm_state)
    y = y + d_skip[:, None] * xs
    y = (y.reshape(b, l, SSD_WIDTH) * jax.nn.silu(z)).reshape(b, l, SSD_GROUPS, SSD_WIDTH // SSD_GROUPS)
    y = y * lax.rsqrt(jnp.mean(y * y, axis=-1, keepdims=True) + EPS)
    y = y.reshape(b, l, SSD_WIDTH) * norm_g
    return y, new_conv, new_state


def _mla_attend(q_lat, q_rope, k_lat, k_rope, q_pos, k_pos):
    b, l, h, c = q_lat.shape
    qb = Q_BLOCK if l % Q_BLOCK == 0 else l
    nb = l // qb
    scale = (MLA_NOPE + MLA_ROPE) ** -0.5

    def block(args):
        ql, qr, qp = args
        s = (jnp.einsum('bqhc,bkc->bhqk', ql, k_lat) + jnp.einsum('bqhr,bkr->bhqk', qr, k_rope)) * scale
        s = jnp.where(k_pos[None, None, None, :] <= qp[None, None, :, None], s, -jnp.inf)
        pr = jax.nn.softmax(s, axis=-1)
        return jnp.einsum('bhqk,bkc->bqhc', pr, k_lat)

    ql = jnp.swapaxes(q_lat.reshape(b, nb, qb, h, c), 0, 1)
    qr = jnp.swapaxes(q_rope.reshape(b, nb, qb, h, MLA_ROPE), 0, 1)
    out = lax.map(block, (ql, qr, q_pos.reshape(nb, qb)))
    return jnp.swapaxes(out, 0, 1).reshape(b, l, h, c)


def _mla_mixer(cq, ckv, kr, past_lat, past_rope, start, g_q, w_uq, g_kv, w_uk, w_uv):
    b, l, _ = cq.shape
    past = past_lat.shape[1]
    q_pos = start + jnp.arange(l)
    q = jnp.einsum('blr,rhd->blhd', _rmsnorm(cq, g_q), w_uq)
    q_nope, q_rope = q[..., :MLA_NOPE], _rope(q[..., MLA_NOPE:], q_pos)
    lat = _rmsnorm(ckv, g_kv)
    k_rope = _rope(kr, q_pos)
    q_lat = jnp.einsum('blhd,chd->blhc', q_nope, w_uk)
    k_lat_all = jnp.concatenate([past_lat, lat], axis=1)
    k_rope_all = jnp.concatenate([past_rope, k_rope], axis=1)
    k_pos = start - past + jnp.arange(past + l)
    o_lat = _mla_attend(q_lat, q_rope, k_lat_all, k_rope_all, q_pos, k_pos)
    o = jnp.einsum('blhc,chv->blhv', o_lat, w_uv).reshape(b, l, MLA_HEADS * MLA_V_DIM)
    return o, lat, k_rope


def _decoder_layer(x3, mod, pool_hist, conv_hist, ssm_state, past_lat, past_rope, start, p):
    b, l, d = x3.shape
    m = lambda i: mod[:, i, None, :]
    u = norm_mod_matmul(x3, p['g_mix'], 1.0 + m(1), m(0), p['w_in'], tn=896, name="in_proj")
    u = u.reshape(b, l, D_IN_PACKED)
    seg = lambda off, width: u[..., off:off + width]
    y_pool, new_pool = _pool_mixer(seg(OFF_POOL, POOL_WIDTH), pool_hist, start, p['w_pool'], p['pool_scale'])
    y_ssd, new_conv, new_ssm = _ssd_mixer(seg(OFF_Z, SSD_WIDTH), seg(OFF_XBC, SSD_CONV_DIM), seg(OFF_DT, SSD_HEADS),
                                          conv_hist, ssm_state, p['conv_w'], p['conv_b'], p['dt_bias'],
                                          p['a_log'], p['d_skip'], p['ssd_norm'])
    y_mla, lat, krope = _mla_mixer(seg(OFF_CQ, MLA_Q_RANK), seg(OFF_CKV, MLA_KV_RANK), seg(OFF_KR, MLA_ROPE),
                                   past_lat, past_rope, start, p['g_q'], p['w_uq'], p['g_kv'], p['w_uk'], p['w_uv'])
    flat = lambda y: y.reshape(b * l, -1).astype(BF16)
    x3 = proj_residual(flat(y_pool), flat(y_ssd), flat(y_mla), *p['w_out'], x3, m(2))
    q, hn = norm_mod_matmul(x3, p['g_ffn'], 1.0 + m(4), m(3), p['peer_wq'], tn=512, emit_hn=True, name="peer_query")
    s0, s1, e0, e1, thr = peer_router(q, p['peer_keys'])
    y = peer_dense(hn, p['peer_u'], p['peer_v'], s0, s1, e0, e1, thr)
    x3 = gated_residual(x3, m(5), y)
    return x3, new_pool, new_conv, new_ssm, lat, krope


def _pack_w_in(w):
    d = w.shape[0]
    dt0 = D_IN_RAW_DT
    dt1 = dt0 + SSD_HEADS
    pad = jnp.zeros((d, D_IN_PACKED - OFF_DT - SSD_HEADS), w.dtype)
    return jnp.concatenate([w[:, :dt0], w[:, dt1:], w[:, dt0:dt1], pad], axis=1).astype(BF16)


def kernel(x_prompt, x_sample, cache_latent, cache_rope, state_ssm, state_conv, state_pool, page_table, c_prompt, c_sample, w_ada, b_ada, ada_table, g_mix, w_in, w_pool, pool_scale, conv_w, conv_b, dt_bias, a_log, d_skip, ssd_norm, g_q, w_uq, g_kv, w_uk, w_uv, w_out, g_ffn, peer_wq, peer_keys, peer_u, peer_v, g_final):
    depth = w_in.shape[0]
    bp = x_prompt.shape[0]
    bs = x_sample.shape[0]
    past_len = page_table.shape[1] * PAGE_SIZE

    n_c = bp + bs
    c_all = jnp.concatenate([c_prompt, c_sample, jnp.zeros((-n_c % 8, D_MODEL), F32)], axis=0)
    mod_all = adaln_base(c_all, w_ada, b_ada)
    mod_p = mod_all[:bp].reshape(bp, N_MOD, D_MODEL)
    mod_s = mod_all[bp:n_c].reshape(bs, N_MOD, D_MODEL)

    zero_pool = jnp.zeros((bp, POOL_HIST, POOL_WIDTH), F32)
    zero_conv = jnp.zeros((bp, SSD_CONV - 1, SSD_CONV_DIM), F32)
    zero_ssm = jnp.zeros((bp, SSD_HEADS, SSD_HEAD_DIM, SSD_STATE), F32)
    no_lat = jnp.zeros((bp, 0, MLA_KV_RANK), F32)
    no_rope = jnp.zeros((bp, 0, MLA_ROPE), F32)

    hp, hs = x_prompt, x_sample
    outs = [[] for _ in range(10)]
    for l in range(depth):
        w_out_l = w_out[l].astype(BF16)
        p = {'g_mix': g_mix[l], 'w_in': _pack_w_in(w_in[l]), 'w_pool': w_pool[l], 'pool_scale': pool_scale[l],
             'conv_w': conv_w[l], 'conv_b': conv_b[l], 'dt_bias': dt_bias[l], 'a_log': a_log[l],
             'd_skip': d_skip[l], 'ssd_norm': ssd_norm[l], 'g_q': g_q[l], 'w_uq': w_uq[l],
             'g_kv': g_kv[l], 'w_uk': w_uk[l], 'w_uv': w_uv[l],
             'w_out': (w_out_l[:POOL_WIDTH], w_out_l[POOL_WIDTH:POOL_WIDTH + SSD_WIDTH],
                       w_out_l[POOL_WIDTH + SSD_WIDTH:]),
             'g_ffn': g_ffn[l], 'peer_wq': peer_wq[l].astype(BF16), 'peer_keys': peer_keys[l].astype(BF16),
             'peer_u': peer_u[l].astype(BF16), 'peer_v': peer_v[l].astype(BF16)}
        hp, np_pool, np_conv, np_ssm, np_lat, np_rope = _decoder_layer(
            hp, mod_p + ada_table[l], zero_pool, zero_conv, zero_ssm, no_lat, no_rope, 0, p)
        past_lat = cache_latent[l, page_table].reshape(bs, past_len, MLA_KV_RANK)
        past_rope = cache_rope[l, page_table].reshape(bs, past_len, MLA_ROPE)
        hs, ns_pool, ns_conv, ns_ssm, ns_lat, ns_rope = _decoder_layer(
            hs, mod_s + ada_table[l], state_pool[l], state_conv[l], state_ssm[l], past_lat, past_rope, past_len, p)
        for lst, val in zip(outs, (np_lat, np_rope, ns_lat, ns_rope, np_ssm, ns_ssm, np_conv, ns_conv,
                                   np_pool, ns_pool)):
            lst.append(val)
    y_prompt = rmsnorm_rows(hp, g_final)
    y_sample = rmsnorm_rows(hs, g_final)
    return (y_prompt, y_sample) + tuple(jnp.stack(o) for o in outs)
```

```python
import functools

import jax
import jax.numpy as jnp
from jax import lax
from jax.experimental import pallas as pl
from jax.experimental.pallas import tpu as pltpu

F32 = jnp.float32
BF16 = jnp.bfloat16
HIGHEST = lax.Precision.HIGHEST

D_MODEL = 4096
N_MOD = 6
EPS = 1e-6
PAGE_SIZE = 128

POOL_WIDTH = D_MODEL // 4
POOL_WINDOWS = (2, 4, 8, 16)
POOL_GROUP = POOL_WIDTH // len(POOL_WINDOWS)
POOL_HIST = max(POOL_WINDOWS) - 1

SSD_WIDTH = 3 * D_MODEL // 8
SSD_HEAD_DIM = 64
SSD_HEADS = SSD_WIDTH // SSD_HEAD_DIM
SSD_GROUPS = 4
SSD_GROUP_HEADS = SSD_HEADS // SSD_GROUPS
SSD_GROUP_WIDTH = SSD_WIDTH // SSD_GROUPS
SSD_STATE = 128
SSD_CONV = 4
SSD_CHUNK = 128
SSD_CONV_DIM = SSD_WIDTH + 2 * SSD_GROUPS * SSD_STATE

MLA_WIDTH = D_MODEL - POOL_WIDTH - SSD_WIDTH
MLA_V_DIM = 128
MLA_HEADS = MLA_WIDTH // MLA_V_DIM
MLA_NOPE = 128
MLA_ROPE = 64
MLA_Q_RANK = 3 * D_MODEL // 16
MLA_KV_RANK = D_MODEL // 16
ROPE_THETA = 10000.0

PEER_HEADS = 8
PEER_NKEYS = 128
PEER_EXPERTS = PEER_NKEYS * PEER_NKEYS
PEER_TOPK = 16
PEER_DKEY = 256

LANES = 128
OFF_Z = 0
OFF_XBC = OFF_Z + SSD_WIDTH
OFF_POOL = OFF_XBC + SSD_CONV_DIM
OFF_CQ = OFF_POOL + POOL_WIDTH
OFF_CKV = OFF_CQ + MLA_Q_RANK
OFF_KR = OFF_CKV + MLA_KV_RANK
OFF_DT = OFF_KR + LANES
D_IN_PACKED = OFF_DT + LANES
ZX_WIDTH = SSD_WIDTH + SSD_CONV_DIM
MLA_SLAB = D_IN_PACKED - OFF_CQ
MLA_QK = MLA_KV_RANK + LANES
MLA_QHEAD = MLA_NOPE + LANES

VMEM_LIMIT = 56 * 1024 * 1024
ROW_TILE = 512

NT_DIMS = (((1,), (1,)), ((), ()))
TN_DIMS = (((0,), (0,)), ((), ()))


def _cparams(n_axes):
    return pltpu.CompilerParams(dimension_semantics=("arbitrary",) * n_axes,
                                vmem_limit_bytes=VMEM_LIMIT)


def _token_tiling(x3):
    g, r, _ = x3.shape
    if r >= ROW_TILE:
        assert r % ROW_TILE == 0
        return 1, ROW_TILE
    tg = min(ROW_TILE // r, 32)
    assert ROW_TILE % r == 0 and g % tg == 0
    return tg, r


def _silu(x):
    return x * jax.nn.sigmoid(x)


BF16_ROWS = 16


def _dot_bf16(a, w):
    m = a.shape[0]
    if m < BF16_ROWS:
        a = jnp.concatenate([a, jnp.zeros((BF16_ROWS - m, a.shape[1]), a.dtype)], axis=0)
    return jnp.dot(a.astype(BF16), w, preferred_element_type=F32)[:m]


def _adaln_kernel(c_ref, w_ref, b_ref, o_ref):
    a = _silu(c_ref[...]).astype(BF16)
    o_ref[...] = jnp.dot(a, w_ref[...].astype(BF16), preferred_element_type=F32) + b_ref[...]


def adaln_base(c, w_ada, b_ada):
    n, d = c.shape
    dout = w_ada.shape[1]
    tn = 512
    return pl.pallas_call(
        _adaln_kernel,
        grid=(dout // tn,),
        in_specs=[pl.BlockSpec((n, d), lambda j: (0, 0)),
                  pl.BlockSpec((d, tn), lambda j: (0, j)),
                  pl.BlockSpec((1, tn), lambda j: (0, j))],
        out_specs=pl.BlockSpec((n, tn), lambda j: (0, j)),
        out_shape=jax.ShapeDtypeStruct((n, dout), F32),
        compiler_params=_cparams(1),
        name="adaln",
    )(c, w_ada, b_ada.reshape(1, dout))


def _norm_mod_matmul_kernel(x_ref, g_ref, sc_ref, sh_ref, w_ref, o_ref, *rest, emit_hn):
    if emit_hn:
        hn_out_ref, hn_s = rest
    else:
        (hn_s,) = rest

    @pl.when(pl.program_id(2) == 0)
    def _():
        x = x_ref[...]
        y = x * lax.rsqrt(jnp.mean(x * x, axis=-1, keepdims=True) + EPS) * g_ref[...]
        hn = (y * sc_ref[...] + sh_ref[...]).reshape(hn_s.shape).astype(BF16)
        hn_s[...] = hn
        if emit_hn:
            hn_out_ref[...] = hn

    o_ref[...] = jnp.dot(hn_s[...], w_ref[...], preferred_element_type=F32)


def norm_mod_matmul(x3, g, scale1p, shift, w, tn, emit_hn=False, name="norm_mod_matmul"):
    gdim, r, d = x3.shape
    dout = w.shape[1]
    tg, rb = _token_tiling(x3)
    tm = tg * rb
    n_r = r // rb
    grid = (gdim // tg, n_r, dout // tn)
    out_shape = [jax.ShapeDtypeStruct((gdim * r, dout), F32)]
    out_specs = [pl.BlockSpec((tm, tn), lambda gi, ri, j: (gi * n_r + ri, j))]
    if emit_hn:
        out_shape.append(jax.ShapeDtypeStruct((gdim * r, d), BF16))
        out_specs.append(pl.BlockSpec((tm, d), lambda gi, ri, j: (gi * n_r + ri, 0)))
    res = pl.pallas_call(
        functools.partial(_norm_mod_matmul_kernel, emit_hn=emit_hn),
        grid=grid,
        in_specs=[pl.BlockSpec((tg, rb, d), lambda gi, ri, j: (gi, ri, 0)),
                  pl.BlockSpec((1, 1, d), lambda gi, ri, j: (0, 0, 0)),
                  pl.BlockSpec((tg, 1, d), lambda gi, ri, j: (gi, 0, 0)),
                  pl.BlockSpec((tg, 1, d), lambda gi, ri, j: (gi, 0, 0)),
                  pl.BlockSpec((d, tn), lambda gi, ri, j: (0, j))],
        out_specs=out_specs,
        out_shape=out_shape,
        scratch_shapes=[pltpu.VMEM((tm, d), BF16)],
        compiler_params=_cparams(3),
        name=name,
    )(x3, g.reshape(1, 1, d), scale1p, shift, w)
    return res if emit_hn else res[0]


def _proj_residual_kernel(yp_ref, ys_ref, ym_ref, w1_ref, w2_ref, w3_ref, x_ref, gate_ref, o_ref):
    acc = jnp.dot(yp_ref[...].astype(BF16), w1_ref[...], preferred_element_type=F32)
    acc += jnp.dot(ys_ref[...].astype(BF16), w2_ref[...], preferred_element_type=F32)
    acc += jnp.dot(ym_ref[...].astype(BF16), w3_ref[...], preferred_element_type=F32)
    o_ref[...] = x_ref[...] + gate_ref[...] * acc.reshape(x_ref.shape)


def proj_residual(y_pool, y_ssd, y_mla, w1, w2, w3, x3, gate):
    gdim, r, d = x3.shape
    tg, rb = _token_tiling(x3)
    tm = tg * rb
    n_r = r // rb
    tn = 512
    row = lambda gi, ri, j: (gi * n_r + ri, 0)
    return pl.pallas_call(
        _proj_residual_kernel,
        grid=(gdim // tg, n_r, d // tn),
        in_specs=[pl.BlockSpec((tm, y_pool.shape[1]), row),
                  pl.BlockSpec((tm, y_ssd.shape[1]), row),
                  pl.BlockSpec((tm, y_mla.shape[1]), row),
                  pl.BlockSpec((w1.shape[0], tn), lambda gi, ri, j: (0, j)),
                  pl.BlockSpec((w2.shape[0], tn), lambda gi, ri, j: (0, j)),
                  pl.BlockSpec((w3.shape[0], tn), lambda gi, ri, j: (0, j)),
                  pl.BlockSpec((tg, rb, tn), lambda gi, ri, j: (gi, ri, j)),
                  pl.BlockSpec((tg, 1, tn), lambda gi, ri, j: (gi, 0, j))],
        out_specs=pl.BlockSpec((tg, rb, tn), lambda gi, ri, j: (gi, ri, j)),
        out_shape=jax.ShapeDtypeStruct(x3.shape, F32),
        compiler_params=_cparams(3),
        name="proj_residual",
    )(y_pool, y_ssd, y_mla, w1, w2, w3, x3, gate)


POOL_PAD = POOL_HIST + 1


def _pool_kernel(u_ref, hist_ref, w_ref, scale_ref, y_ref, newhist_ref, ext_ref, *, start, tl):
    t = pl.program_id(1)

    @pl.when(t == 0)
    def _():
        ext_ref[0:1, :] = jnp.zeros((1, POOL_WIDTH), F32)
        ext_ref[1:POOL_PAD, :] = hist_ref[0]

    ext_ref[POOL_PAD:POOL_PAD + tl, :] = u_ref[...]
    pos = start + t * tl + lax.broadcasted_iota(jnp.int32, (tl, POOL_GROUP), 0)
    for g, w in enumerate(POOL_WINDOWS):
        c0 = g * POOL_GROUP
        cur = ext_ref[POOL_PAD:POOL_PAD + tl, c0:c0 + POOL_GROUP]
        acc = cur
        for k in range(1, w):
            acc = acc + ext_ref[POOL_PAD - k:POOL_PAD - k + tl, c0:c0 + POOL_GROUP]
        cnt = jnp.minimum(pos + 1, w).astype(F32)
        d = acc / cnt - cur
        y = _dot_bf16(d, w_ref[g]) * scale_ref[:, c0:c0 + POOL_GROUP]
        y_ref[:, c0:c0 + POOL_GROUP] = y.astype(y_ref.dtype)
    tail = ext_ref[tl + 1:tl + POOL_PAD, :]
    newhist_ref[0] = tail
    ext_ref[1:POOL_PAD, :] = tail


def pool_mixer(u, hist, start, w_pool, pool_scale, out_dtype):
    b = hist.shape[0]
    l = u.shape[0] // b
    tl = min(l, 256)
    n_t = l // tl
    return pl.pallas_call(
        functools.partial(_pool_kernel, start=start, tl=tl),
        grid=(b, n_t),
        in_specs=[pl.BlockSpec((tl, POOL_WIDTH), lambda bi, t: (bi * n_t + t, OFF_POOL // POOL_WIDTH)),
                  pl.BlockSpec((1, POOL_HIST, POOL_WIDTH), lambda bi, t: (bi, 0, 0)),
                  pl.BlockSpec(w_pool.shape, lambda bi, t: (0, 0, 0)),
                  pl.BlockSpec((1, POOL_WIDTH), lambda bi, t: (0, 0))],
        out_specs=[pl.BlockSpec((tl, POOL_WIDTH), lambda bi, t: (bi * n_t + t, 0)),
                   pl.BlockSpec((1, POOL_HIST, POOL_WIDTH), lambda bi, t: (bi, 0, 0))],
        out_shape=[jax.ShapeDtypeStruct((b * l, POOL_WIDTH), out_dtype),
                   jax.ShapeDtypeStruct((b, POOL_HIST, POOL_WIDTH), F32)],
        scratch_shapes=[pltpu.VMEM((POOL_PAD + tl, POOL_WIDTH), F32)],
        compiler_params=_cparams(2),
        name="pool_mixer",
    )(u, hist, w_pool, pool_scale.reshape(1, POOL_WIDTH))


CONV_PAD = 8


def _ssd_conv_silu(ext_ref, xbc, hist, convw_ref, convb_ref, q):
    if hist is not None:
        ext_ref[CONV_PAD - 3:CONV_PAD, :] = hist
    ext_ref[CONV_PAD:CONV_PAD + q, :] = xbc
    conv = convb_ref[...]
    for k in range(SSD_CONV):
        lo = CONV_PAD - (SSD_CONV - 1) + k
        conv = conv + convw_ref[k:k + 1, :] * ext_ref[lo:lo + q, :]
    tail = ext_ref[CONV_PAD + q - 3:CONV_PAD + q, :]
    return _silu(conv), tail


def _softplus(x):
    return jnp.maximum(x, 0.0) + jnp.log(1.0 + jnp.exp(-jnp.abs(x)))


def _ssd_gate_norm(y, z, normg_ref, y_ref):
    y = y * _silu(z)
    for g in range(SSD_GROUPS):
        seg = y[:, g * SSD_GROUP_WIDTH:(g + 1) * SSD_GROUP_WIDTH]
        seg = seg * lax.rsqrt(jnp.mean(seg * seg, axis=-1, keepdims=True) + EPS)
        seg = seg * normg_ref[:, g * SSD_GROUP_WIDTH:(g + 1) * SSD_GROUP_WIDTH]
        y_ref[:, g * SSD_GROUP_WIDTH:(g + 1) * SSD_GROUP_WIDTH] = seg.astype(y_ref.dtype)


def _ssd_chunk_kernel(zx_ref, dt_ref, convh_ref, ssm_ref, convw_ref, convb_ref, dtb_ref, alog_ref,
                      dskip_ref, normg_ref, y_ref, newconv_ref, newssm_ref, ext_ref, state_ref, ybuf_ref):
    q = SSD_CHUNK
    c = pl.program_id(1)

    @pl.when(c == 0)
    def _():
        ext_ref[CONV_PAD - 3:CONV_PAD, :] = convh_ref[0]
        state_ref[...] = ssm_ref[0].reshape(state_ref.shape)

    xa, tail = _ssd_conv_silu(ext_ref, zx_ref[:, SSD_WIDTH:], None, convw_ref, convb_ref, q)
    newconv_ref[0] = tail
    ext_ref[CONV_PAD - 3:CONV_PAD, :] = tail
    xs = xa[:, :SSD_WIDTH]

    dt = _softplus(dt_ref[...] + dtb_ref[...])
    a = dt * -jnp.exp(alog_ref[...])
    li = lax.broadcasted_iota(jnp.int32, (q, q), 0)
    si = lax.broadcasted_iota(jnp.int32, (q, q), 1)
    tril = si <= li
    a_cs = jnp.dot(tril.astype(F32), a, precision=HIGHEST, preferred_element_type=F32)
    a_cs_t = lax.dot_general((li == si).astype(F32), a_cs, NT_DIMS, precision=HIGHEST,
                             preferred_element_type=F32)
    a_last = a_cs[q - 1:q, :]

    for g in range(SSD_GROUPS):
        b0 = SSD_WIDTH + g * SSD_STATE
        c0 = SSD_WIDTH + SSD_GROUPS * SSD_STATE + g * SSD_STATE
        bg = xa[:, b0:b0 + SSD_STATE].astype(BF16)
        cg = xa[:, c0:c0 + SSD_STATE].astype(BF16)
        cb = lax.dot_general(cg, bg, NT_DIMS, preferred_element_type=F32)
        r0 = g * SSD_GROUP_WIDTH
        y_off = lax.dot_general(cg, state_ref[r0:r0 + SSD_GROUP_WIDTH, :].astype(BF16), NT_DIMS,
                                preferred_element_type=F32)
        xdd = []
        for r in range(SSD_GROUP_HEADS):
            h = g * SSD_GROUP_HEADS + r
            col = a_cs[:, h:h + 1]
            xd = xs[:, h * SSD_HEAD_DIM:(h + 1) * SSD_HEAD_DIM] * dt[:, h:h + 1]
            decay = jnp.exp(jnp.where(tril, col - a_cs_t[h:h + 1, :], -jnp.inf))
            y_diag = jnp.dot((cb * decay).astype(BF16), xd.astype(BF16), preferred_element_type=F32)
            ybuf_ref[:, h * SSD_HEAD_DIM:(h + 1) * SSD_HEAD_DIM] = (
                y_diag + y_off[:, r * SSD_HEAD_DIM:(r + 1) * SSD_HEAD_DIM] * jnp.exp(col))
            xdd.append(xd * jnp.exp(a_last[:, h:h + 1] - col))
        new = lax.dot_general(jnp.concatenate(xdd, axis=1).astype(BF16), bg, TN_DIMS,
                              preferred_element_type=F32)
        for r in range(SSD_GROUP_HEADS):
            h = g * SSD_GROUP_HEADS + r
            rows = slice(h * SSD_HEAD_DIM, (h + 1) * SSD_HEAD_DIM)
            state_ref[rows, :] = (state_ref[rows, :] * jnp.exp(a_last[:, h:h + 1])
                                  + new[r * SSD_HEAD_DIM:(r + 1) * SSD_HEAD_DIM])

    y = ybuf_ref[...] + dskip_ref[...] * xs
    _ssd_gate_norm(y, zx_ref[:, :SSD_WIDTH], normg_ref, y_ref)

    @pl.when(c == pl.num_programs(1) - 1)
    def _():
        newssm_ref[0] = state_ref[...].reshape(newssm_ref.shape[1:])


def _ssd_step_kernel(zx_ref, dt_ref, convh_ref, ssm_ref, convw_ref, convb_ref, dtb_ref, alog_ref,
                     dskip_ref, normg_ref, y_ref, newconv_ref, newssm_ref, ext_ref):
    q = zx_ref.shape[0]
    xa, tail = _ssd_conv_silu(ext_ref, zx_ref[:, SSD_WIDTH:], convh_ref[0], convw_ref, convb_ref, q)
    newconv_ref[0] = tail
    xs = xa[:, :SSD_WIDTH]

    dt = _softplus(dt_ref[...] + dtb_ref[...])
    a = dt * -jnp.exp(alog_ref[...])
    rows128 = lax.broadcasted_iota(jnp.int32, (q, LANES), 0)
    a_cs = jnp.zeros((q, LANES), F32)
    for s in range(q):
        a_cs = a_cs + jnp.where(rows128 >= s, a[s:s + 1, :], 0.0)
    a_last = a_cs[q - 1:q, :]
    state = ssm_ref[0].reshape(SSD_WIDTH, SSD_STATE)
    pad_rows = SSD_CHUNK - q
    gn = SSD_GROUPS * SSD_STATE
    bm = xa[:, SSD_WIDTH:SSD_WIDTH + gn]
    cm = xa[:, SSD_WIDTH + gn:]

    group_of_lane = lax.broadcasted_iota(jnp.int32, (gn, LANES), 0) // SSD_STATE
    head = lax.broadcasted_iota(jnp.int32, (gn, LANES), 1)
    group_sum = (head // SSD_GROUP_HEADS == group_of_lane).astype(F32)
    spread = (lax.broadcasted_iota(jnp.int32, (LANES, SSD_WIDTH), 1) // SSD_HEAD_DIM
              == lax.broadcasted_iota(jnp.int32, (LANES, SSD_WIDTH), 0)).astype(F32)
    cb = jnp.dot(jnp.concatenate([cm * bm[s:s + 1, :] for s in range(q)], axis=0), group_sum,
                 precision=HIGHEST, preferred_element_type=F32)
    weights = [jnp.where(rows128 >= s, jnp.exp(a_cs - a_cs[s:s + 1, :]), 0.0) * cb[s * q:(s + 1) * q]
               for s in range(q)]
    per_head = jnp.concatenate([dt, jnp.exp(a_cs), jnp.exp(a_last - a_cs)] + weights, axis=0)
    wide = jnp.dot(per_head, spread, precision=HIGHEST, preferred_element_type=F32)
    xd = xs * wide[0:q]
    y = dskip_ref[...] * xs
    for s in range(q):
        y = y + wide[(3 + s) * q:(4 + s) * q] * xd[s:s + 1, :]
    xdd = xd * wide[2 * q:3 * q]

    y_off = []
    for g in range(SSD_GROUPS):
        r0 = g * SSD_GROUP_WIDTH
        old = state[r0:r0 + SSD_GROUP_WIDTH, :]
        cg = cm[:, g * SSD_STATE:(g + 1) * SSD_STATE]
        cg16 = jnp.concatenate([cg, jnp.zeros((BF16_ROWS - q, SSD_STATE), F32)], axis=0).astype(BF16)
        y_off.append(lax.dot_general(cg16, old.astype(BF16), NT_DIMS, preferred_element_type=F32)[:q])
        xdd_p = jnp.concatenate([xdd[:, r0:r0 + SSD_GROUP_WIDTH],
                                 jnp.zeros((pad_rows, SSD_GROUP_WIDTH), F32)], axis=0).astype(BF16)
        bg_p = jnp.concatenate([bm[:, g * SSD_STATE:(g + 1) * SSD_STATE],
                                jnp.zeros((pad_rows, SSD_STATE), F32)], axis=0).astype(BF16)
        new = lax.dot_general(xdd_p, bg_p, TN_DIMS, preferred_element_type=F32)
        for r in range(SSD_GROUP_HEADS):
            h = g * SSD_GROUP_HEADS + r
            newssm_ref[0, h] = (old[r * SSD_HEAD_DIM:(r + 1) * SSD_HEAD_DIM] * jnp.exp(a_last[:, h:h + 1])
                                + new[r * SSD_HEAD_DIM:(r + 1) * SSD_HEAD_DIM])

    y = y + jnp.concatenate(y_off, axis=1) * wide[q:2 * q]
    _ssd_gate_norm(y, zx_ref[:, :SSD_WIDTH], normg_ref, y_ref)


def ssd_mixer(u, conv_hist, ssm_state, conv_w, conv_b, dt_bias, a_log, d_skip, norm_g, out_dtype):
    b = conv_hist.shape[0]
    l = u.shape[0] // b
    pad = lambda v: jnp.pad(v, (0, LANES - SSD_HEADS)).reshape(1, LANES)
    params = (conv_w, conv_b.reshape(1, SSD_CONV_DIM), pad(dt_bias), pad(a_log),
              jnp.repeat(d_skip, SSD_HEAD_DIM).reshape(1, SSD_WIDTH), norm_g.reshape(1, SSD_WIDTH))
    chunked = l % SSD_CHUNK == 0
    q = SSD_CHUNK if chunked else l
    n_c = l // q
    const2 = lambda bi, c: (0, 0)
    in_specs = [pl.BlockSpec((q, ZX_WIDTH), lambda bi, c: (bi * n_c + c, 0)),
                pl.BlockSpec((q, LANES), lambda bi, c: (bi * n_c + c, OFF_DT // LANES)),
                pl.BlockSpec((1, SSD_CONV - 1, SSD_CONV_DIM), lambda bi, c: (bi, 0, 0)),
                pl.BlockSpec((1,) + ssm_state.shape[1:], lambda bi, c: (bi, 0, 0, 0))]
    in_specs += [pl.BlockSpec(p.shape, const2) for p in params]
    scratch = [pltpu.VMEM((CONV_PAD + q, SSD_CONV_DIM), F32)]
    if chunked:
        scratch += [pltpu.VMEM((SSD_WIDTH, SSD_STATE), F32), pltpu.VMEM((q, SSD_WIDTH), F32)]
    return pl.pallas_call(
        _ssd_chunk_kernel if chunked else _ssd_step_kernel,
        grid=(b, n_c),
        in_specs=in_specs,
        out_specs=[pl.BlockSpec((q, SSD_WIDTH), lambda bi, c: (bi * n_c + c, 0)),
                   pl.BlockSpec((1, SSD_CONV - 1, SSD_CONV_DIM), lambda bi, c: (bi, 0, 0)),
                   pl.BlockSpec((1,) + ssm_state.shape[1:], lambda bi, c: (bi, 0, 0, 0))],
        out_shape=[jax.ShapeDtypeStruct((b * l, SSD_WIDTH), out_dtype),
                   jax.ShapeDtypeStruct(conv_hist.shape, F32),
                   jax.ShapeDtypeStruct(ssm_state.shape, F32)],
        scratch_shapes=scratch,
        compiler_params=_cparams(2),
        name="ssd_chunked" if chunked else "ssd_step",
    )(u, u, conv_hist, ssm_state, *params)


def _rotate(slab, rope_ref):
    return (slab * rope_ref[0]
            + pltpu.roll(slab, LANES - MLA_ROPE // 2, 1) * rope_ref[1]
            + pltpu.roll(slab, MLA_ROPE // 2, 1) * rope_ref[2])


def _mla_prep_kernel(slab_ref, rope_ref, gq_ref, gkv_ref, wuq_ref, wuk_ref,
                     q_ref, kfull_ref, lat_t_ref, lat_ref, krope_ref):
    def norm(x, g_ref):
        return x * lax.rsqrt(jnp.mean(x * x, axis=-1, keepdims=True) + EPS) * g_ref[...]

    qn = norm(slab_ref[:, :MLA_Q_RANK], gq_ref).astype(BF16)
    lat = norm(slab_ref[:, MLA_Q_RANK:MLA_Q_RANK + MLA_KV_RANK], gkv_ref)
    k_rot = _rotate(slab_ref[:, OFF_KR - OFF_CQ:OFF_KR - OFF_CQ + LANES], rope_ref)
    lat_ref[...] = lat
    krope_ref[...] = k_rot[:, :MLA_ROPE]
    kfull_ref[...] = jnp.concatenate([lat, k_rot], axis=-1).astype(BF16)
    lat_t_ref[...] = lat.T.astype(BF16)
    for h in range(MLA_HEADS):
        qh = jnp.dot(qn, wuq_ref[:, h * MLA_QHEAD:(h + 1) * MLA_QHEAD], preferred_element_type=F32)
        q_lat = jnp.dot(qh[:, :MLA_NOPE].astype(BF16), wuk_ref[h], preferred_element_type=F32)
        q_ref[h] = jnp.concatenate([q_lat, _rotate(qh[:, MLA_NOPE:], rope_ref)], axis=-1).astype(q_ref.dtype)


def _rope_tables(pos):
    half = MLA_ROPE // 2
    inv_freq = ROPE_THETA ** (-jnp.arange(half, dtype=F32) / half)
    ang = pos.astype(F32)[:, None] * inv_freq[None, :]
    cos, sin = jnp.cos(ang), jnp.sin(ang)
    z = jnp.zeros_like(cos)
    return jnp.stack([jnp.concatenate([cos, cos, z, z], axis=-1),
                      jnp.concatenate([-sin, z, z, z], axis=-1),
                      jnp.concatenate([z, sin, z, z], axis=-1)])


def mla_prep(u, seq_len, start, g_q, g_kv, w_uq_packed, w_uk_t, q_dtype):
    n = u.shape[0]
    tl = ROW_TILE
    reps = max(tl // seq_len, 1)
    rope = _rope_tables(jnp.tile(start + jnp.arange(seq_len), reps))
    n_rope = rope.shape[1] // tl
    return pl.pallas_call(
        _mla_prep_kernel,
        grid=(n // tl,),
        in_specs=[pl.BlockSpec((tl, MLA_SLAB), lambda i: (i, OFF_CQ // MLA_SLAB)),
                  pl.BlockSpec((3, tl, LANES), lambda i: (0, i % n_rope, 0)),
                  pl.BlockSpec((1, MLA_Q_RANK), lambda i: (0, 0)),
                  pl.BlockSpec((1, MLA_KV_RANK), lambda i: (0, 0)),
                  pl.BlockSpec(w_uq_packed.shape, lambda i: (0, 0)),
                  pl.BlockSpec(w_uk_t.shape, lambda i: (0, 0, 0))],
        out_specs=[pl.BlockSpec((MLA_HEADS, tl, MLA_QK), lambda i: (0, i, 0)),
                   pl.BlockSpec((tl, MLA_QK), lambda i: (i, 0)),
                   pl.BlockSpec((MLA_KV_RANK, tl), lambda i: (0, i)),
                   pl.BlockSpec((tl, MLA_KV_RANK), lambda i: (i, 0)),
                   pl.BlockSpec((tl, MLA_ROPE), lambda i: (i, 0))],
        out_shape=[jax.ShapeDtypeStruct((MLA_HEADS, n, MLA_QK), q_dtype),
                   jax.ShapeDtypeStruct((n, MLA_QK), BF16),
                   jax.ShapeDtypeStruct((MLA_KV_RANK, n), BF16),
                   jax.ShapeDtypeStruct((n, MLA_KV_RANK), F32),
                   jax.ShapeDtypeStruct((n, MLA_ROPE), F32)],
        compiler_params=_cparams(1),
        name="mla_prep",
    )(u, rope, g_q.reshape(1, MLA_Q_RANK), g_kv.reshape(1, MLA_KV_RANK), w_uq_packed, w_uk_t)


MLA_SCALE = (MLA_NOPE + MLA_ROPE) ** -0.5


def _softmax_update(s, v_terms, m_ref, l_ref, acc_ref):
    m_new = jnp.maximum(m_ref[...], jnp.max(s, axis=-1, keepdims=True))
    alpha = jnp.exp(m_ref[...] - m_new)
    p = jnp.exp(s - m_new)
    l_ref[...] = alpha * l_ref[...] + jnp.sum(p, axis=-1, keepdims=True)
    acc_ref[...] = alpha * acc_ref[...] + v_terms(p.astype(BF16))
    m_ref[...] = m_new


def _softmax_init(m_ref, l_ref, acc_ref):
    m_ref[...] = jnp.full(m_ref.shape, -jnp.inf, F32)
    l_ref[...] = jnp.zeros(l_ref.shape, F32)
    acc_ref[...] = jnp.zeros(acc_ref.shape, F32)


def _value_up_projection(acc_ref, l_ref, wuv_ref, y_ref, tq):
    for h in range(MLA_HEADS):
        rows = slice(h * tq, (h + 1) * tq)
        y = _dot_bf16(acc_ref[rows, :] / l_ref[rows, :], wuv_ref[h])
        y_ref[:, h * MLA_V_DIM:(h + 1) * MLA_V_DIM] = y.astype(y_ref.dtype)


PROMPT_ATTN_TILE = 256


def _mla_prompt_kernel(q_ref, k_ref, kt_ref, wuvt_ref, y_ref, m_ref, l_ref, acc_ref):
    tq = PROMPT_ATTN_TILE
    qi, ki = pl.program_id(1), pl.program_id(2)

    @pl.when(ki == 0)
    def _():
        _softmax_init(m_ref, l_ref, acc_ref)

    def step(on_diagonal):
        q = q_ref[...].reshape(MLA_HEADS * tq, MLA_QK)
        s = lax.dot_general(k_ref[...], q, NT_DIMS, preferred_element_type=F32) * MLA_SCALE
        if on_diagonal:
            k_pos = lax.broadcasted_iota(jnp.int32, s.shape, 0)
            q_pos = lax.broadcasted_iota(jnp.int32, s.shape, 1) & (tq - 1)
            s = jnp.where(k_pos <= q_pos, s, -jnp.inf)
        m_new = jnp.maximum(m_ref[...], jnp.max(s, axis=0, keepdims=True))
        alpha = jnp.exp(m_ref[...] - m_new)
        p = jnp.exp(s - m_new)
        l_ref[...] = alpha * l_ref[...] + jnp.sum(p, axis=0, keepdims=True)
        acc_ref[...] = alpha * acc_ref[...] + jnp.dot(kt_ref[...], p.astype(BF16),
                                                      preferred_element_type=F32)
        m_ref[...] = m_new

    pl.when(ki < qi)(lambda: step(False))
    pl.when(ki == qi)(lambda: step(True))

    @pl.when(ki == pl.num_programs(2) - 1)
    def _():
        o_t = (acc_ref[...] / l_ref[...]).astype(BF16)
        y_t = jnp.concatenate(
            [jnp.dot(wuvt_ref[h], o_t[:, h * tq:(h + 1) * tq], preferred_element_type=F32)
             for h in range(MLA_HEADS)], axis=0)
        y_ref[...] = y_t.T.astype(y_ref.dtype)


def mla_attend_prompt(q, k_full, k_lat_t, w_uv_t, batch):
    n = k_full.shape[0]
    t = PROMPT_ATTN_TILE
    n_t = n // batch // t
    rows = MLA_HEADS * t
    return pl.pallas_call(
        _mla_prompt_kernel,
        grid=(batch, n_t, n_t),
        in_specs=[pl.BlockSpec((MLA_HEADS, t, MLA_QK), lambda b, qi, ki: (0, b * n_t + qi, 0)),
                  pl.BlockSpec((t, MLA_QK), lambda b, qi, ki: (b * n_t + jnp.minimum(ki, qi), 0)),
                  pl.BlockSpec((MLA_KV_RANK, t), lambda b, qi, ki: (0, b * n_t + jnp.minimum(ki, qi))),
                  pl.BlockSpec(w_uv_t.shape, lambda b, qi, ki: (0, 0, 0))],
        out_specs=pl.BlockSpec((t, MLA_WIDTH), lambda b, qi, ki: (b * n_t + qi, 0)),
        out_shape=jax.ShapeDtypeStruct((n, MLA_WIDTH), BF16),
        scratch_shapes=[pltpu.VMEM((1, rows), F32), pltpu.VMEM((1, rows), F32),
                        pltpu.VMEM((MLA_KV_RANK, rows), F32)],
        compiler_params=_cparams(3),
        name="mla_attend_prompt",
    )(q, k_full, k_lat_t, w_uv_t)


PAGES_PER_STEP = 16


def _mla_sample_kernel(pt_ref, q_ref, *refs):
    del pt_ref
    pp = PAGES_PER_STEP
    lat_refs, rope_refs = refs[:pp], refs[pp:2 * pp]
    lat_new_ref, rope_new_ref, wuv_ref, y_ref, kbuf_ref, m_ref, l_ref, acc_ref = refs[2 * pp:]
    tq = q_ref.shape[1]
    j = pl.program_id(1)
    rope_end = MLA_KV_RANK + MLA_ROPE

    @pl.when(j == 0)
    def _():
        _softmax_init(m_ref, l_ref, acc_ref)
        kbuf_ref[:, rope_end:] = jnp.zeros((kbuf_ref.shape[0], MLA_QK - rope_end), BF16)

    q = q_ref[...].reshape(MLA_HEADS * tq, MLA_QK).astype(BF16)

    for i in range(pp):
        rows = slice(i * PAGE_SIZE, (i + 1) * PAGE_SIZE)
        kbuf_ref[rows, :MLA_KV_RANK] = lat_refs[i][...].astype(BF16)
        kbuf_ref[rows, MLA_KV_RANK:rope_end] = rope_refs[i][...].astype(BF16)
    s = lax.dot_general(q, kbuf_ref[...], NT_DIMS, preferred_element_type=F32) * MLA_SCALE
    _softmax_update(s, lambda p: jnp.dot(p, kbuf_ref[:, :MLA_KV_RANK], preferred_element_type=F32),
                    m_ref, l_ref, acc_ref)

    @pl.when(j == pl.num_programs(1) - 1)
    def _():
        pad = lambda x: jnp.concatenate([x, jnp.zeros((PAGE_SIZE - tq, x.shape[1]), F32)], axis=0).astype(BF16)
        lat_n = pad(lat_new_ref[...])
        s_n = (lax.dot_general(q[:, :MLA_KV_RANK], lat_n, NT_DIMS, preferred_element_type=F32)
               + lax.dot_general(q[:, MLA_KV_RANK:rope_end], pad(rope_new_ref[...]), NT_DIMS,
                                 preferred_element_type=F32)) * MLA_SCALE
        q_pos = lax.broadcasted_iota(jnp.int32, s_n.shape, 0) & (tq - 1)
        k_pos = lax.broadcasted_iota(jnp.int32, s_n.shape, 1)
        s_n = jnp.where(k_pos <= q_pos, s_n, -jnp.inf)
        _softmax_update(s_n, lambda p: jnp.dot(p, lat_n, preferred_element_type=F32), m_ref, l_ref, acc_ref)
        _value_up_projection(acc_ref, l_ref, wuv_ref, y_ref, tq)


def mla_attend_sample(q, lat_new, rope_new, cache_latent, cache_rope, page_table, layer, w_uv):
    bs, n_pages = page_table.shape
    tq = lat_new.shape[0] // bs
    pp = PAGES_PER_STEP
    page = lambda i: (lambda b, j, pt: (layer, pt[b, j * pp + i], 0, 0))
    in_specs = [pl.BlockSpec((MLA_HEADS, tq, MLA_QK), lambda b, j, pt: (0, b, 0))]
    in_specs += [pl.BlockSpec((None, None, PAGE_SIZE, MLA_KV_RANK), page(i)) for i in range(pp)]
    in_specs += [pl.BlockSpec((None, None, PAGE_SIZE, MLA_ROPE), page(i)) for i in range(pp)]
    in_specs += [pl.BlockSpec((tq, MLA_KV_RANK), lambda b, j, pt: (b, 0)),
                 pl.BlockSpec((tq, MLA_ROPE), lambda b, j, pt: (b, 0)),
                 pl.BlockSpec(w_uv.shape, lambda b, j, pt: (0, 0, 0))]
    rows = MLA_HEADS * tq
    return pl.pallas_call(
        _mla_sample_kernel,
        grid_spec=pltpu.PrefetchScalarGridSpec(
            num_scalar_prefetch=1,
            grid=(bs, n_pages // pp),
            in_specs=in_specs,
            out_specs=pl.BlockSpec((tq, MLA_WIDTH), lambda b, j, pt: (b, 0)),
            scratch_shapes=[pltpu.VMEM((pp * PAGE_SIZE, MLA_QK), BF16),
                            pltpu.VMEM((rows, 1), F32), pltpu.VMEM((rows, 1), F32),
                            pltpu.VMEM((rows, MLA_KV_RANK), F32)]),
        out_shape=jax.ShapeDtypeStruct((bs * tq, MLA_WIDTH), F32),
        compiler_params=_cparams(2),
        name="mla_attend_sample",
    )(page_table, q, *([cache_latent] * pp), *([cache_rope] * pp), lat_new, rope_new, w_uv)


ROUTER_TILE = 256


def _top16_rows(s):
    iota = lax.broadcasted_iota(jnp.int32, s.shape, 0)
    rem = s
    vals = []
    for _ in range(PEER_TOPK):
        m = jnp.max(rem, axis=0, keepdims=True)
        idx = jnp.min(jnp.where(rem == m, iota, PEER_NKEYS), axis=0, keepdims=True)
        rem = jnp.where(iota == idx, -jnp.inf, rem)
        vals.append(m)
    return rem != s, jnp.concatenate(vals, axis=0)


def _router_kernel(q_ref, keys_ref, s0_ref, s1_ref, e0_ref, e1_ref, thr_ref):
    half = PEER_DKEY // 2

    def per_head(h, carry):
        def scores(side):
            col = pl.multiple_of(h * PEER_DKEY + side * half, half)
            qh = q_ref[:, pl.ds(col, half)].astype(BF16)
            return lax.dot_general(keys_ref[h, side], qh, NT_DIMS, preferred_element_type=F32)

        s0, s1 = scores(0), scores(1)
        mask0, sv0 = _top16_rows(s0)
        mask1, sv1 = _top16_rows(s1)
        cand = jnp.concatenate([sv0[a:a + 1] + sv1 for a in range(PEER_TOPK)], axis=0)
        cmax = sv0[0:1] + sv1[0:1]
        rem = cand
        cum = jnp.zeros_like(cmax)
        thr = cmax
        for _ in range(PEER_TOPK):
            m = jnp.max(rem, axis=0, keepdims=True)
            eq = rem == m
            thr = jnp.where(cum < PEER_TOPK, m, thr)
            cum = cum + jnp.sum(eq.astype(F32), axis=0, keepdims=True)
            rem = jnp.where(eq, -jnp.inf, rem)
        z = jnp.sum(jnp.where(cand >= thr, jnp.exp(cand - cmax), 0.0), axis=0, keepdims=True)
        s0_ref[h] = s0
        s1_ref[h] = s1
        e0_ref[h] = jnp.where(mask0, jnp.exp(s0 - sv0[0:1]), 0.0)
        e1_ref[h] = jnp.where(mask1, jnp.exp(s1 - sv1[0:1]), 0.0) / z
        thr_ref[pl.ds(h, 1), :] = thr
        return carry

    lax.fori_loop(0, PEER_HEADS, per_head, 0)


def peer_router(q, keys_bf16):
    n = q.shape[0]
    t = ROUTER_TILE
    big = jax.ShapeDtypeStruct((PEER_HEADS, PEER_NKEYS, n), F32)
    big_spec = pl.BlockSpec((PEER_HEADS, PEER_NKEYS, t), lambda i: (0, 0, i))
    return pl.pallas_call(
        _router_kernel,
        grid=(n // t,),
        in_specs=[pl.BlockSpec((t, PEER_HEADS * PEER_DKEY), lambda i: (i, 0)),
                  pl.BlockSpec(keys_bf16.shape, lambda i: (0, 0, 0, 0))],
        out_specs=[big_spec, big_spec, big_spec, big_spec,
                   pl.BlockSpec((PEER_HEADS, t), lambda i: (0, i))],
        out_shape=[big, big, big, big, jax.ShapeDtypeStruct((PEER_HEADS, n), F32)],
        compiler_params=_cparams(1),
        name="peer_router",
    )(q, keys_bf16)


PEER_TOKEN_TILE = 512
PEER_EXPERT_TILE = 512


def _gelu_exact(a):
    return 0.5 * a * (1.0 + lax.erf(a * (2.0 ** -0.5)))


def _peer_dense_kernel(hn_ref, u_ref, v_ref, s0_ref, s1_ref, e0_ref, e1_ref, thr_ref, o_ref):
    e = pl.program_id(1)
    rows_per_tile = PEER_EXPERT_TILE // PEER_NKEYS

    @pl.when(e == 0)
    def _():
        o_ref[...] = jnp.zeros_like(o_ref)

    act_t = lax.dot_general(u_ref[...], hn_ref[...], NT_DIMS, preferred_element_type=F32)
    coef = []
    for ib in range(rows_per_tile):
        i = e * rows_per_tile + ib
        gate = jnp.zeros((PEER_NKEYS, act_t.shape[1]), F32)
        for h in range(PEER_HEADS):
            picked = (s0_ref[h, pl.ds(i, 1), :] + s1_ref[h]) >= thr_ref[pl.ds(h, 1), :]
            gate = gate + jnp.where(picked, e0_ref[h, pl.ds(i, 1), :] * e1_ref[h], 0.0)
        a = act_t[ib * PEER_NKEYS:(ib + 1) * PEER_NKEYS]
        coef.append((gate * _gelu_exact(a)).astype(BF16))
    coef_t = jnp.concatenate(coef, axis=0)
    o_ref[...] += lax.dot_general(coef_t, v_ref[...], TN_DIMS, preferred_element_type=F32)


def peer_dense(hn, u_bf16, v_bf16, s0, s1, e0, e1, thr):
    n, d = hn.shape
    tm, te = PEER_TOKEN_TILE, PEER_EXPERT_TILE
    once = pl.Buffered(1)
    big_spec = pl.BlockSpec((PEER_HEADS, PEER_NKEYS, tm), lambda i, e: (0, 0, i), pipeline_mode=once)
    return pl.pallas_call(
        _peer_dense_kernel,
        grid=(n // tm, PEER_EXPERTS // te),
        in_specs=[pl.BlockSpec((tm, d), lambda i, e: (i, 0), pipeline_mode=once),
                  pl.BlockSpec((te, d), lambda i, e: (e, 0)),
                  pl.BlockSpec((te, d), lambda i, e: (e, 0)),
                  big_spec, big_spec, big_spec, big_spec,
                  pl.BlockSpec((PEER_HEADS, tm), lambda i, e: (0, i), pipeline_mode=once)],
        out_specs=pl.BlockSpec((tm, d), lambda i, e: (i, 0)),
        out_shape=jax.ShapeDtypeStruct((n, d), F32),
        compiler_params=_cparams(2),
        name="peer_dense",
    )(hn, u_bf16, v_bf16, s0, s1, e0, e1, thr)


def _gated_residual_kernel(x_ref, gate_ref, y_ref, o_ref):
    o_ref[...] = x_ref[...] + gate_ref[...] * y_ref[...].reshape(x_ref.shape)


def gated_residual(x3, gate, y):
    gdim, r, d = x3.shape
    tg, rb = _token_tiling(x3)
    n_r = r // rb
    return pl.pallas_call(
        _gated_residual_kernel,
        grid=(gdim // tg, n_r),
        in_specs=[pl.BlockSpec((tg, rb, d), lambda gi, ri: (gi, ri, 0)),
                  pl.BlockSpec((tg, 1, d), lambda gi, ri: (gi, 0, 0)),
                  pl.BlockSpec((tg * rb, d), lambda gi, ri: (gi * n_r + ri, 0))],
        out_specs=pl.BlockSpec((tg, rb, d), lambda gi, ri: (gi, ri, 0)),
        out_shape=jax.ShapeDtypeStruct(x3.shape, F32),
        compiler_params=_cparams(2),
        name="gated_residual",
    )(x3, gate, y)


def _rmsnorm_kernel(x_ref, g_ref, o_ref):
    x = x_ref[...]
    o_ref[...] = x * lax.rsqrt(jnp.mean(x * x, axis=-1, keepdims=True) + EPS) * g_ref[...]


def rmsnorm_rows(x3, g):
    gdim, r, d = x3.shape
    tg, rb = _token_tiling(x3)
    return pl.pallas_call(
        _rmsnorm_kernel,
        grid=(gdim // tg, r // rb),
        in_specs=[pl.BlockSpec((tg, rb, d), lambda gi, ri: (gi, ri, 0)),
                  pl.BlockSpec((1, 1, d), lambda gi, ri: (0, 0, 0))],
        out_specs=pl.BlockSpec((tg, rb, d), lambda gi, ri: (gi, ri, 0)),
        out_shape=jax.ShapeDtypeStruct(x3.shape, F32),
        compiler_params=_cparams(2),
        name="final_rmsnorm",
    )(x3, g.reshape(1, 1, d))


def _decoder_layer(x3, mod, pool_hist, conv_hist, ssm_state, caches, start, p):
    b, l, d = x3.shape
    m = lambda i: mod[:, i, None, :]
    is_prompt = caches is None
    act_dtype = BF16 if is_prompt else F32
    u = norm_mod_matmul(x3, p['g_mix'], 1.0 + m(1), m(0), p['w_in'], tn=640, name="in_proj")
    y_pool, new_pool = pool_mixer(u, pool_hist, start, p['w_pool'], p['pool_scale'], act_dtype)
    y_ssd, new_conv, new_ssm = ssd_mixer(u, conv_hist, ssm_state, p['conv_w'], p['conv_b'], p['dt_bias'],
                                         p['a_log'], p['d_skip'], p['ssd_norm'], act_dtype)
    q, k_full, k_lat_t, lat, krope = mla_prep(u, l, start, p['g_q'], p['g_kv'], p['w_uq'], p['w_uk'], act_dtype)
    if is_prompt:
        y_mla = mla_attend_prompt(q, k_full, k_lat_t, p['w_uv_t'], b)
    else:
        y_mla = mla_attend_sample(q, lat, krope, *caches, p['w_uv'])
    x3 = proj_residual(y_pool, y_ssd, y_mla, *p['w_out'], x3, m(2))
    q, hn = norm_mod_matmul(x3, p['g_ffn'], 1.0 + m(4), m(3), p['peer_wq'], tn=512, emit_hn=True, name="peer_query")
    s0, s1, e0, e1, thr = peer_router(q, p['peer_keys'])
    y = peer_dense(hn, p['peer_u'], p['peer_v'], s0, s1, e0, e1, thr)
    x3 = gated_residual(x3, m(5), y)
    return (x3, new_pool, new_conv, new_ssm,
            lat.reshape(b, l, MLA_KV_RANK), krope.reshape(b, l, MLA_ROPE))


def _pack_w_in(w):
    d = w.shape[0]
    sizes = (POOL_WIDTH, SSD_WIDTH, SSD_CONV_DIM, SSD_HEADS, MLA_Q_RANK, MLA_KV_RANK, MLA_ROPE)
    offs = [0]
    for s in sizes:
        offs.append(offs[-1] + s)
    pool, z, xbc, dt, cq, ckv, kr = (w[:, offs[i]:offs[i + 1]] for i in range(len(sizes)))
    zeros = lambda k: jnp.zeros((d, k), w.dtype)
    packed = jnp.concatenate([z, xbc, pool, cq, ckv, kr, zeros(LANES - MLA_ROPE),
                              dt, zeros(LANES - SSD_HEADS)], axis=1)
    assert packed.shape[1] == D_IN_PACKED
    return packed.astype(BF16)


def _pack_w_uq(w):
    r = w.shape[0]
    padded = jnp.concatenate([w, jnp.zeros((r, MLA_HEADS, MLA_QHEAD - w.shape[2]), w.dtype)], axis=-1)
    return padded.reshape(r, MLA_HEADS * MLA_QHEAD).astype(BF16)


def kernel(x_prompt, x_sample, cache_latent, cache_rope, state_ssm, state_conv, state_pool, page_table, c_prompt, c_sample, w_ada, b_ada, ada_table, g_mix, w_in, w_pool, pool_scale, conv_w, conv_b, dt_bias, a_log, d_skip, ssd_norm, g_q, w_uq, g_kv, w_uk, w_uv, w_out, g_ffn, peer_wq, peer_keys, peer_u, peer_v, g_final):
    depth = w_in.shape[0]
    bp = x_prompt.shape[0]
    bs = x_sample.shape[0]
    past_len = page_table.shape[1] * PAGE_SIZE

    n_c = bp + bs
    c_all = jnp.concatenate([c_prompt, c_sample, jnp.zeros((-n_c % 8, D_MODEL), F32)], axis=0)
    mod_all = adaln_base(c_all, w_ada, b_ada)
    mod_p = mod_all[:bp].reshape(bp, N_MOD, D_MODEL)
    mod_s = mod_all[bp:n_c].reshape(bs, N_MOD, D_MODEL)

    zero_pool = jnp.zeros((bp, POOL_HIST, POOL_WIDTH), F32)
    zero_conv = jnp.zeros((bp, SSD_CONV - 1, SSD_CONV_DIM), F32)
    zero_ssm = jnp.zeros((bp, SSD_HEADS, SSD_HEAD_DIM, SSD_STATE), F32)

    hp, hs = x_prompt, x_sample
    outs = [[] for _ in range(10)]
    for l in range(depth):
        w_out_l = w_out[l].astype(BF16)
        p = {'g_mix': g_mix[l], 'w_in': _pack_w_in(w_in[l]), 'w_pool': w_pool[l].astype(BF16),
             'pool_scale': pool_scale[l], 'conv_w': conv_w[l], 'conv_b': conv_b[l], 'dt_bias': dt_bias[l],
             'a_log': a_log[l], 'd_skip': d_skip[l], 'ssd_norm': ssd_norm[l], 'g_q': g_q[l],
             'w_uq': _pack_w_uq(w_uq[l]), 'g_kv': g_kv[l],
             'w_uk': jnp.transpose(w_uk[l], (1, 2, 0)).astype(BF16),
             'w_uv': jnp.transpose(w_uv[l], (1, 0, 2)).astype(BF16),
             'w_uv_t': jnp.transpose(w_uv[l], (1, 2, 0)).astype(BF16),
             'w_out': (w_out_l[:POOL_WIDTH], w_out_l[POOL_WIDTH:POOL_WIDTH + SSD_WIDTH],
                       w_out_l[POOL_WIDTH + SSD_WIDTH:]),
             'g_ffn': g_ffn[l], 'peer_wq': peer_wq[l].astype(BF16), 'peer_keys': peer_keys[l].astype(BF16),
             'peer_u': peer_u[l].astype(BF16), 'peer_v': peer_v[l].astype(BF16)}
        hp, np_pool, np_conv, np_ssm, np_lat, np_rope = _decoder_layer(
            hp, mod_p + ada_table[l], zero_pool, zero_conv, zero_ssm, None, 0, p)
        hs, ns_pool, ns_conv, ns_ssm, ns_lat, ns_rope = _decoder_layer(
            hs, mod_s + ada_table[l], state_pool[l], state_conv[l], state_ssm[l],
            (cache_latent, cache_rope, page_table, l), past_len, p)
        for lst, val in zip(outs, (np_lat, np_rope, ns_lat, ns_rope, np_ssm, ns_ssm, np_conv, ns_conv,
                                   np_pool, ns_pool)):
            lst.append(val)
    y_prompt = rmsnorm_rows(hp, g_final)
    y_sample = rmsnorm_rows(hs, g_final)
    return (y_prompt, y_sample) + tuple(jnp.stack(o) for o in outs)
```

```python
import functools

import jax
import jax.numpy as jnp
from jax import lax
from jax.experimental import pallas as pl
from jax.experimental.pallas import tpu as pltpu

F32 = jnp.float32
BF16 = jnp.bfloat16
HIGHEST = lax.Precision.HIGHEST

D_MODEL = 4096
N_MOD = 6
EPS = 1e-6
PAGE_SIZE = 128

POOL_WIDTH = D_MODEL // 4
POOL_WINDOWS = (2, 4, 8, 16)
POOL_GROUP = POOL_WIDTH // len(POOL_WINDOWS)
POOL_HIST = max(POOL_WINDOWS) - 1

SSD_WIDTH = 3 * D_MODEL // 8
SSD_HEAD_DIM = 64
SSD_HEADS = SSD_WIDTH // SSD_HEAD_DIM
SSD_GROUPS = 4
SSD_GROUP_HEADS = SSD_HEADS // SSD_GROUPS
SSD_GROUP_WIDTH = SSD_WIDTH // SSD_GROUPS
SSD_STATE = 128
SSD_CONV = 4
SSD_CHUNK = 128
SSD_CONV_DIM = SSD_WIDTH + 2 * SSD_GROUPS * SSD_STATE

MLA_WIDTH = D_MODEL - POOL_WIDTH - SSD_WIDTH
MLA_V_DIM = 128
MLA_HEADS = MLA_WIDTH // MLA_V_DIM
MLA_NOPE = 128
MLA_ROPE = 64
MLA_Q_RANK = 3 * D_MODEL // 16
MLA_KV_RANK = D_MODEL // 16
ROPE_THETA = 10000.0

PEER_HEADS = 8
PEER_NKEYS = 128
PEER_EXPERTS = PEER_NKEYS * PEER_NKEYS
PEER_TOPK = 16
PEER_DKEY = 256

LANES = 128
OFF_Z = 0
OFF_XBC = OFF_Z + SSD_WIDTH
OFF_POOL = OFF_XBC + SSD_CONV_DIM
OFF_CQ = OFF_POOL + POOL_WIDTH
OFF_CKV = OFF_CQ + MLA_Q_RANK
OFF_KR = OFF_CKV + MLA_KV_RANK
OFF_DT = OFF_KR + LANES
D_IN_PACKED = OFF_DT + LANES
ZX_WIDTH = SSD_WIDTH + SSD_CONV_DIM
MLA_SLAB = D_IN_PACKED - OFF_CQ
MLA_QK = MLA_KV_RANK + LANES
MLA_QHEAD = MLA_NOPE + LANES

VMEM_LIMIT = 56 * 1024 * 1024
ROW_TILE = 512

NT_DIMS = (((1,), (1,)), ((), ()))
TN_DIMS = (((0,), (0,)), ((), ()))


def _cparams(n_axes):
    return pltpu.CompilerParams(dimension_semantics=("arbitrary",) * n_axes,
                                vmem_limit_bytes=VMEM_LIMIT)


def _token_tiling(x3):
    g, r, _ = x3.shape
    if r >= ROW_TILE:
        assert r % ROW_TILE == 0
        return 1, ROW_TILE
    tg = min(ROW_TILE // r, 32)
    assert ROW_TILE % r == 0 and g % tg == 0
    return tg, r


def _silu(x):
    return x * jax.nn.sigmoid(x)


BF16_ROWS = 16


def _dot_bf16(a, w):
    m = a.shape[0]
    if m < BF16_ROWS:
        a = jnp.concatenate([a, jnp.zeros((BF16_ROWS - m, a.shape[1]), a.dtype)], axis=0)
    return jnp.dot(a.astype(BF16), w, preferred_element_type=F32)[:m]


def _adaln_kernel(c_ref, w_ref, b_ref, o_ref):
    a = _silu(c_ref[...]).astype(BF16)
    o_ref[...] = jnp.dot(a, w_ref[...].astype(BF16), preferred_element_type=F32) + b_ref[...]


def adaln_base(c, w_ada, b_ada):
    n, d = c.shape
    dout = w_ada.shape[1]
    tn = 512
    return pl.pallas_call(
        _adaln_kernel,
        grid=(dout // tn,),
        in_specs=[pl.BlockSpec((n, d), lambda j: (0, 0)),
                  pl.BlockSpec((d, tn), lambda j: (0, j)),
                  pl.BlockSpec((1, tn), lambda j: (0, j))],
        out_specs=pl.BlockSpec((n, tn), lambda j: (0, j)),
        out_shape=jax.ShapeDtypeStruct((n, dout), F32),
        compiler_params=_cparams(1),
        name="adaln",
    )(c, w_ada, b_ada.reshape(1, dout))


def _norm_mod_matmul_kernel(x_ref, g_ref, sc_ref, sh_ref, w_ref, o_ref, *rest, emit_hn):
    if emit_hn:
        hn_out_ref, hn_s = rest
    else:
        (hn_s,) = rest

    @pl.when(pl.program_id(2) == 0)
    def _():
        x = x_ref[...]
        y = x * lax.rsqrt(jnp.mean(x * x, axis=-1, keepdims=True) + EPS) * g_ref[...]
        hn = (y * sc_ref[...] + sh_ref[...]).reshape(hn_s.shape).astype(BF16)
        hn_s[...] = hn
        if emit_hn:
            hn_out_ref[...] = hn

    o_ref[...] = jnp.dot(hn_s[...], w_ref[...], preferred_element_type=F32)


def norm_mod_matmul(x3, g, scale1p, shift, w, tn, emit_hn=False, name="norm_mod_matmul"):
    gdim, r, d = x3.shape
    dout = w.shape[1]
    tg, rb = _token_tiling(x3)
    tm = tg * rb
    n_r = r // rb
    grid = (gdim // tg, n_r, dout // tn)
    out_shape = [jax.ShapeDtypeStruct((gdim * r, dout), F32)]
    out_specs = [pl.BlockSpec((tm, tn), lambda gi, ri, j: (gi * n_r + ri, j))]
    if emit_hn:
        out_shape.append(jax.ShapeDtypeStruct((gdim * r, d), BF16))
        out_specs.append(pl.BlockSpec((tm, d), lambda gi, ri, j: (gi * n_r + ri, 0)))
    res = pl.pallas_call(
        functools.partial(_norm_mod_matmul_kernel, emit_hn=emit_hn),
        grid=grid,
        in_specs=[pl.BlockSpec((tg, rb, d), lambda gi, ri, j: (gi, ri, 0)),
                  pl.BlockSpec((1, 1, d), lambda gi, ri, j: (0, 0, 0)),
                  pl.BlockSpec((tg, 1, d), lambda gi, ri, j: (gi, 0, 0)),
                  pl.BlockSpec((tg, 1, d), lambda gi, ri, j: (gi, 0, 0)),
                  pl.BlockSpec((d, tn), lambda gi, ri, j: (0, j))],
        out_specs=out_specs,
        out_shape=out_shape,
        scratch_shapes=[pltpu.VMEM((tm, d), BF16)],
        compiler_params=_cparams(3),
        name=name,
    )(x3, g.reshape(1, 1, d), scale1p, shift, w)
    return res if emit_hn else res[0]


def _proj_residual_kernel(yp_ref, ys_ref, ym_ref, w1_ref, w2_ref, w3_ref, x_ref, gate_ref, o_ref):
    acc = jnp.dot(yp_ref[...].astype(BF16), w1_ref[...], preferred_element_type=F32)
    acc += jnp.dot(ys_ref[...].astype(BF16), w2_ref[...], preferred_element_type=F32)
    acc += jnp.dot(ym_ref[...].astype(BF16), w3_ref[...], preferred_element_type=F32)
    o_ref[...] = x_ref[...] + gate_ref[...] * acc.reshape(x_ref.shape)


def proj_residual(y_pool, y_ssd, y_mla, w1, w2, w3, x3, gate):
    gdim, r, d = x3.shape
    tg, rb = _token_tiling(x3)
    tm = tg * rb
    n_r = r // rb
    tn = 512
    row = lambda gi, ri, j: (gi * n_r + ri, 0)
    return pl.pallas_call(
        _proj_residual_kernel,
        grid=(gdim // tg, n_r, d // tn),
        in_specs=[pl.BlockSpec((tm, y_pool.shape[1]), row),
                  pl.BlockSpec((tm, y_ssd.shape[1]), row),
                  pl.BlockSpec((tm, y_mla.shape[1]), row),
                  pl.BlockSpec((w1.shape[0], tn), lambda gi, ri, j: (0, j)),
                  pl.BlockSpec((w2.shape[0], tn), lambda gi, ri, j: (0, j)),
                  pl.BlockSpec((w3.shape[0], tn), lambda gi, ri, j: (0, j)),
                  pl.BlockSpec((tg, rb, tn), lambda gi, ri, j: (gi, ri, j)),
                  pl.BlockSpec((tg, 1, tn), lambda gi, ri, j: (gi, 0, j))],
        out_specs=pl.BlockSpec((tg, rb, tn), lambda gi, ri, j: (gi, ri, j)),
        out_shape=jax.ShapeDtypeStruct(x3.shape, F32),
        compiler_params=_cparams(3),
        name="proj_residual",
    )(y_pool, y_ssd, y_mla, w1, w2, w3, x3, gate)


POOL_PAD = POOL_HIST + 1


def _pool_kernel(u_ref, hist_ref, w_ref, scale_ref, y_ref, newhist_ref, ext_ref, *, start, tl):
    t = pl.program_id(1)

    @pl.when(t == 0)
    def _():
        ext_ref[0:1, :] = jnp.zeros((1, POOL_WIDTH), F32)
        ext_ref[1:POOL_PAD, :] = hist_ref[0]

    ext_ref[POOL_PAD:POOL_PAD + tl, :] = u_ref[...]
    pos = start + t * tl + lax.broadcasted_iota(jnp.int32, (tl, POOL_GROUP), 0)
    for g, w in enumerate(POOL_WINDOWS):
        c0 = g * POOL_GROUP
        cur = ext_ref[POOL_PAD:POOL_PAD + tl, c0:c0 + POOL_GROUP]
        acc = cur
        for k in range(1, w):
            acc = acc + ext_ref[POOL_PAD - k:POOL_PAD - k + tl, c0:c0 + POOL_GROUP]
        cnt = jnp.minimum(pos + 1, w).astype(F32)
        d = acc / cnt - cur
        y = _dot_bf16(d, w_ref[g]) * scale_ref[:, c0:c0 + POOL_GROUP]
        y_ref[:, c0:c0 + POOL_GROUP] = y.astype(y_ref.dtype)
    tail = ext_ref[tl + 1:tl + POOL_PAD, :]
    newhist_ref[0] = tail
    ext_ref[1:POOL_PAD, :] = tail


def pool_mixer(u, hist, start, w_pool, pool_scale, out_dtype):
    b = hist.shape[0]
    l = u.shape[0] // b
    tl = min(l, 256)
    n_t = l // tl
    return pl.pallas_call(
        functools.partial(_pool_kernel, start=start, tl=tl),
        grid=(b, n_t),
        in_specs=[pl.BlockSpec((tl, POOL_WIDTH), lambda bi, t: (bi * n_t + t, OFF_POOL // POOL_WIDTH)),
                  pl.BlockSpec((1, POOL_HIST, POOL_WIDTH), lambda bi, t: (bi, 0, 0)),
                  pl.BlockSpec(w_pool.shape, lambda bi, t: (0, 0, 0)),
                  pl.BlockSpec((1, POOL_WIDTH), lambda bi, t: (0, 0))],
        out_specs=[pl.BlockSpec((tl, POOL_WIDTH), lambda bi, t: (bi * n_t + t, 0)),
                   pl.BlockSpec((1, POOL_HIST, POOL_WIDTH), lambda bi, t: (bi, 0, 0))],
        out_shape=[jax.ShapeDtypeStruct((b * l, POOL_WIDTH), out_dtype),
                   jax.ShapeDtypeStruct((b, POOL_HIST, POOL_WIDTH), F32)],
        scratch_shapes=[pltpu.VMEM((POOL_PAD + tl, POOL_WIDTH), F32)],
        compiler_params=_cparams(2),
        name="pool_mixer",
    )(u, hist, w_pool, pool_scale.reshape(1, POOL_WIDTH))


CONV_PAD = 8


def _ssd_conv_silu(ext_ref, xbc, hist, convw_ref, convb_ref, q):
    if hist is not None:
        ext_ref[CONV_PAD - 3:CONV_PAD, :] = hist
    ext_ref[CONV_PAD:CONV_PAD + q, :] = xbc
    conv = convb_ref[...]
    for k in range(SSD_CONV):
        lo = CONV_PAD - (SSD_CONV - 1) + k
        conv = conv + convw_ref[k:k + 1, :] * ext_ref[lo:lo + q, :]
    tail = ext_ref[CONV_PAD + q - 3:CONV_PAD + q, :]
    return _silu(conv), tail


def _softplus(x):
    return jnp.maximum(x, 0.0) + jnp.log(1.0 + jnp.exp(-jnp.abs(x)))


def _ssd_gate_norm(y, z, normg_ref, y_ref):
    y = y * _silu(z)
    for g in range(SSD_GROUPS):
        seg = y[:, g * SSD_GROUP_WIDTH:(g + 1) * SSD_GROUP_WIDTH]
        seg = seg * lax.rsqrt(jnp.mean(seg * seg, axis=-1, keepdims=True) + EPS)
        seg = seg * normg_ref[:, g * SSD_GROUP_WIDTH:(g + 1) * SSD_GROUP_WIDTH]
        y_ref[:, g * SSD_GROUP_WIDTH:(g + 1) * SSD_GROUP_WIDTH] = seg.astype(y_ref.dtype)


def _ssd_chunk_kernel(zx_ref, dt_ref, convh_ref, ssm_ref, convw_ref, convb_ref, dtb_ref, alog_ref,
                      dskip_ref, normg_ref, y_ref, newconv_ref, newssm_ref, ext_ref, state_ref, ybuf_ref):
    q = SSD_CHUNK
    c = pl.program_id(1)

    @pl.when(c == 0)
    def _():
        ext_ref[CONV_PAD - 3:CONV_PAD, :] = convh_ref[0]
        state_ref[...] = ssm_ref[0].reshape(state_ref.shape)

    xa, tail = _ssd_conv_silu(ext_ref, zx_ref[:, SSD_WIDTH:], None, convw_ref, convb_ref, q)
    newconv_ref[0] = tail
    ext_ref[CONV_PAD - 3:CONV_PAD, :] = tail
    xs = xa[:, :SSD_WIDTH]

    dt = _softplus(dt_ref[...] + dtb_ref[...])
    a = dt * -jnp.exp(alog_ref[...])
    li = lax.broadcasted_iota(jnp.int32, (q, q), 0)
    si = lax.broadcasted_iota(jnp.int32, (q, q), 1)
    tril = si <= li
    a_cs = jnp.dot(tril.astype(F32), a, precision=HIGHEST, preferred_element_type=F32)
    a_cs_t = lax.dot_general((li == si).astype(F32), a_cs, NT_DIMS, precision=HIGHEST,
                             preferred_element_type=F32)
    a_last = a_cs[q - 1:q, :]

    for g in range(SSD_GROUPS):
        b0 = SSD_WIDTH + g * SSD_STATE
        c0 = SSD_WIDTH + SSD_GROUPS * SSD_STATE + g * SSD_STATE
        bg = xa[:, b0:b0 + SSD_STATE].astype(BF16)
        cg = xa[:, c0:c0 + SSD_STATE].astype(BF16)
        cb = lax.dot_general(cg, bg, NT_DIMS, preferred_element_type=F32)
        r0 = g * SSD_GROUP_WIDTH
        y_off = lax.dot_general(cg, state_ref[r0:r0 + SSD_GROUP_WIDTH, :].astype(BF16), NT_DIMS,
                                preferred_element_type=F32)
        xdd = []
        for r in range(SSD_GROUP_HEADS):
            h = g * SSD_GROUP_HEADS + r
            col = a_cs[:, h:h + 1]
            xd = xs[:, h * SSD_HEAD_DIM:(h + 1) * SSD_HEAD_DIM] * dt[:, h:h + 1]
            decay = jnp.exp(jnp.where(tril, col - a_cs_t[h:h + 1, :], -jnp.inf))
            y_diag = jnp.dot((cb * decay).astype(BF16), xd.astype(BF16), preferred_element_type=F32)
            ybuf_ref[:, h * SSD_HEAD_DIM:(h + 1) * SSD_HEAD_DIM] = (
                y_diag + y_off[:, r * SSD_HEAD_DIM:(r + 1) * SSD_HEAD_DIM] * jnp.exp(col))
            xdd.append(xd * jnp.exp(a_last[:, h:h + 1] - col))
        new = lax.dot_general(jnp.concatenate(xdd, axis=1).astype(BF16), bg, TN_DIMS,
                              preferred_element_type=F32)
        for r in range(SSD_GROUP_HEADS):
            h = g * SSD_GROUP_HEADS + r
            rows = slice(h * SSD_HEAD_DIM, (h + 1) * SSD_HEAD_DIM)
            state_ref[rows, :] = (state_ref[rows, :] * jnp.exp(a_last[:, h:h + 1])
                                  + new[r * SSD_HEAD_DIM:(r + 1) * SSD_HEAD_DIM])

    y = ybuf_ref[...] + dskip_ref[...] * xs
    _ssd_gate_norm(y, zx_ref[:, :SSD_WIDTH], normg_ref, y_ref)

    @pl.when(c == pl.num_programs(1) - 1)
    def _():
        newssm_ref[0] = state_ref[...].reshape(newssm_ref.shape[1:])


def _ssd_step_kernel(zx_ref, dt_ref, convh_ref, ssm_ref, convw_ref, convb_ref, dtb_ref, alog_ref,
                     dskip_ref, normg_ref, y_ref, newconv_ref, newssm_ref, ext_ref):
    q = zx_ref.shape[0]
    xa, tail = _ssd_conv_silu(ext_ref, zx_ref[:, SSD_WIDTH:], convh_ref[0], convw_ref, convb_ref, q)
    newconv_ref[0] = tail
    xs = xa[:, :SSD_WIDTH]

    dt = _softplus(dt_ref[...] + dtb_ref[...])
    a = dt * -jnp.exp(alog_ref[...])
    rows128 = lax.broadcasted_iota(jnp.int32, (q, LANES), 0)
    a_cs = jnp.zeros((q, LANES), F32)
    for s in range(q):
        a_cs = a_cs + jnp.where(rows128 >= s, a[s:s + 1, :], 0.0)
    a_last = a_cs[q - 1:q, :]
    state = ssm_ref[0].reshape(SSD_WIDTH, SSD_STATE)
    pad_rows = SSD_CHUNK - q
    gn = SSD_GROUPS * SSD_STATE
    bm = xa[:, SSD_WIDTH:SSD_WIDTH + gn]
    cm = xa[:, SSD_WIDTH + gn:]

    group_of_lane = lax.broadcasted_iota(jnp.int32, (gn, LANES), 0) // SSD_STATE
    head = lax.broadcasted_iota(jnp.int32, (gn, LANES), 1)
    group_sum = (head // SSD_GROUP_HEADS == group_of_lane).astype(F32)
    spread = (lax.broadcasted_iota(jnp.int32, (LANES, SSD_WIDTH), 1) // SSD_HEAD_DIM
              == lax.broadcasted_iota(jnp.int32, (LANES, SSD_WIDTH), 0)).astype(F32)
    cb = jnp.dot(jnp.concatenate([cm * bm[s:s + 1, :] for s in range(q)], axis=0), group_sum,
                 precision=HIGHEST, preferred_element_type=F32)
    weights = [jnp.where(rows128 >= s, jnp.exp(a_cs - a_cs[s:s + 1, :]), 0.0) * cb[s * q:(s + 1) * q]
               for s in range(q)]
    per_head = jnp.concatenate([dt, jnp.exp(a_cs), jnp.exp(a_last - a_cs)] + weights, axis=0)
    wide = jnp.dot(per_head, spread, precision=HIGHEST, preferred_element_type=F32)
    xd = xs * wide[0:q]
    y = dskip_ref[...] * xs
    for s in range(q):
        y = y + wide[(3 + s) * q:(4 + s) * q] * xd[s:s + 1, :]
    xdd = xd * wide[2 * q:3 * q]

    y_off = []
    for g in range(SSD_GROUPS):
        r0 = g * SSD_GROUP_WIDTH
        old = state[r0:r0 + SSD_GROUP_WIDTH, :]
        cg = cm[:, g * SSD_STATE:(g + 1) * SSD_STATE]
        cg16 = jnp.concatenate([cg, jnp.zeros((BF16_ROWS - q, SSD_STATE), F32)], axis=0).astype(BF16)
        y_off.append(lax.dot_general(cg16, old.astype(BF16), NT_DIMS, preferred_element_type=F32)[:q])
        xdd_p = jnp.concatenate([xdd[:, r0:r0 + SSD_GROUP_WIDTH],
                                 jnp.zeros((pad_rows, SSD_GROUP_WIDTH), F32)], axis=0).astype(BF16)
        bg_p = jnp.concatenate([bm[:, g * SSD_STATE:(g + 1) * SSD_STATE],
                                jnp.zeros((pad_rows, SSD_STATE), F32)], axis=0).astype(BF16)
        new = lax.dot_general(xdd_p, bg_p, TN_DIMS, preferred_element_type=F32)
        for r in range(SSD_GROUP_HEADS):
            h = g * SSD_GROUP_HEADS + r
            newssm_ref[0, h] = (old[r * SSD_HEAD_DIM:(r + 1) * SSD_HEAD_DIM] * jnp.exp(a_last[:, h:h + 1])
                                + new[r * SSD_HEAD_DIM:(r + 1) * SSD_HEAD_DIM])

    y = y + jnp.concatenate(y_off, axis=1) * wide[q:2 * q]
    _ssd_gate_norm(y, zx_ref[:, :SSD_WIDTH], normg_ref, y_ref)


def ssd_mixer(u, conv_hist, ssm_state, conv_w, conv_b, dt_bias, a_log, d_skip, norm_g, out_dtype, layer=None):
    conv_shape, ssm_shape = conv_hist.shape[-3:], ssm_state.shape[-4:]
    b = conv_shape[0]
    l = u.shape[0] // b
    if layer is None:
        conv_spec = pl.BlockSpec((1,) + conv_shape[1:], lambda bi, c: (bi, 0, 0))
        ssm_spec = pl.BlockSpec((1,) + ssm_shape[1:], lambda bi, c: (bi, 0, 0, 0))
    else:
        conv_spec = pl.BlockSpec((None, 1) + conv_shape[1:], lambda bi, c: (layer, bi, 0, 0))
        ssm_spec = pl.BlockSpec((None, 1) + ssm_shape[1:], lambda bi, c: (layer, bi, 0, 0, 0))
    pad = lambda v: jnp.pad(v, (0, LANES - SSD_HEADS)).reshape(1, LANES)
    params = (conv_w, conv_b.reshape(1, SSD_CONV_DIM), pad(dt_bias), pad(a_log),
              jnp.repeat(d_skip, SSD_HEAD_DIM).reshape(1, SSD_WIDTH), norm_g.reshape(1, SSD_WIDTH))
    chunked = l % SSD_CHUNK == 0
    q = SSD_CHUNK if chunked else l
    n_c = l // q
    const2 = lambda bi, c: (0, 0)
    in_specs = [pl.BlockSpec((q, ZX_WIDTH), lambda bi, c: (bi * n_c + c, 0)),
                pl.BlockSpec((q, LANES), lambda bi, c: (bi * n_c + c, OFF_DT // LANES)),
                conv_spec, ssm_spec]
    in_specs += [pl.BlockSpec(p.shape, const2) for p in params]
    scratch = [pltpu.VMEM((CONV_PAD + q, SSD_CONV_DIM), F32)]
    if chunked:
        scratch += [pltpu.VMEM((SSD_WIDTH, SSD_STATE), F32), pltpu.VMEM((q, SSD_WIDTH), F32)]
    return pl.pallas_call(
        _ssd_chunk_kernel if chunked else _ssd_step_kernel,
        grid=(b, n_c),
        in_specs=in_specs,
        out_specs=[pl.BlockSpec((q, SSD_WIDTH), lambda bi, c: (bi * n_c + c, 0)),
                   pl.BlockSpec((1,) + conv_shape[1:], lambda bi, c: (bi, 0, 0)),
                   pl.BlockSpec((1,) + ssm_shape[1:], lambda bi, c: (bi, 0, 0, 0))],
        out_shape=[jax.ShapeDtypeStruct((b * l, SSD_WIDTH), out_dtype),
                   jax.ShapeDtypeStruct(conv_shape, F32),
                   jax.ShapeDtypeStruct(ssm_shape, F32)],
        scratch_shapes=scratch,
        compiler_params=_cparams(2),
        name="ssd_chunked" if chunked else "ssd_step",
    )(u, u, conv_hist, ssm_state, *params)


def _rotate(slab, rope_ref):
    return (slab * rope_ref[0]
            + pltpu.roll(slab, LANES - MLA_ROPE // 2, 1) * rope_ref[1]
            + pltpu.roll(slab, MLA_ROPE // 2, 1) * rope_ref[2])


def _mla_prep_kernel(slab_ref, rope_ref, gq_ref, gkv_ref, wuq_ref, wuk_ref,
                     q_ref, kfull_ref, lat_t_ref, lat_ref, krope_ref):
    def norm(x, g_ref):
        return x * lax.rsqrt(jnp.mean(x * x, axis=-1, keepdims=True) + EPS) * g_ref[...]

    qn = norm(slab_ref[:, :MLA_Q_RANK], gq_ref).astype(BF16)
    lat = norm(slab_ref[:, MLA_Q_RANK:MLA_Q_RANK + MLA_KV_RANK], gkv_ref)
    k_rot = _rotate(slab_ref[:, OFF_KR - OFF_CQ:OFF_KR - OFF_CQ + LANES], rope_ref)
    lat_ref[...] = lat
    krope_ref[...] = k_rot[:, :MLA_ROPE]
    kfull_ref[...] = jnp.concatenate([lat, k_rot], axis=-1).astype(BF16)
    lat_t_ref[...] = lat.T.astype(BF16)
    for h in range(MLA_HEADS):
        qh = jnp.dot(qn, wuq_ref[:, h * MLA_QHEAD:(h + 1) * MLA_QHEAD], preferred_element_type=F32)
        q_lat = jnp.dot(qh[:, :MLA_NOPE].astype(BF16), wuk_ref[h], preferred_element_type=F32)
        q_ref[h] = jnp.concatenate([q_lat, _rotate(qh[:, MLA_NOPE:], rope_ref)], axis=-1).astype(q_ref.dtype)


def _rope_tables(pos):
    half = MLA_ROPE // 2
    inv_freq = ROPE_THETA ** (-jnp.arange(half, dtype=F32) / half)
    ang = pos.astype(F32)[:, None] * inv_freq[None, :]
    cos, sin = jnp.cos(ang), jnp.sin(ang)
    z = jnp.zeros_like(cos)
    return jnp.stack([jnp.concatenate([cos, cos, z, z], axis=-1),
                      jnp.concatenate([-sin, z, z, z], axis=-1),
                      jnp.concatenate([z, sin, z, z], axis=-1)])


def mla_prep(u, seq_len, start, g_q, g_kv, w_uq_packed, w_uk_t, q_dtype):
    n = u.shape[0]
    tl = ROW_TILE
    reps = max(tl // seq_len, 1)
    rope = _rope_tables(jnp.tile(start + jnp.arange(seq_len), reps))
    n_rope = rope.shape[1] // tl
    return pl.pallas_call(
        _mla_prep_kernel,
        grid=(n // tl,),
        in_specs=[pl.BlockSpec((tl, MLA_SLAB), lambda i: (i, OFF_CQ // MLA_SLAB)),
                  pl.BlockSpec((3, tl, LANES), lambda i: (0, i % n_rope, 0)),
                  pl.BlockSpec((1, MLA_Q_RANK), lambda i: (0, 0)),
                  pl.BlockSpec((1, MLA_KV_RANK), lambda i: (0, 0)),
                  pl.BlockSpec(w_uq_packed.shape, lambda i: (0, 0)),
                  pl.BlockSpec(w_uk_t.shape, lambda i: (0, 0, 0))],
        out_specs=[pl.BlockSpec((MLA_HEADS, tl, MLA_QK), lambda i: (0, i, 0)),
                   pl.BlockSpec((tl, MLA_QK), lambda i: (i, 0)),
                   pl.BlockSpec((MLA_KV_RANK, tl), lambda i: (0, i)),
                   pl.BlockSpec((tl, MLA_KV_RANK), lambda i: (i, 0)),
                   pl.BlockSpec((tl, MLA_ROPE), lambda i: (i, 0))],
        out_shape=[jax.ShapeDtypeStruct((MLA_HEADS, n, MLA_QK), q_dtype),
                   jax.ShapeDtypeStruct((n, MLA_QK), BF16),
                   jax.ShapeDtypeStruct((MLA_KV_RANK, n), BF16),
                   jax.ShapeDtypeStruct((n, MLA_KV_RANK), F32),
                   jax.ShapeDtypeStruct((n, MLA_ROPE), F32)],
        compiler_params=_cparams(1),
        name="mla_prep",
    )(u, rope, g_q.reshape(1, MLA_Q_RANK), g_kv.reshape(1, MLA_KV_RANK), w_uq_packed, w_uk_t)


MLA_SCALE = (MLA_NOPE + MLA_ROPE) ** -0.5


def _softmax_update(s, v_terms, m_ref, l_ref, acc_ref):
    m_new = jnp.maximum(m_ref[...], jnp.max(s, axis=-1, keepdims=True))
    alpha = jnp.exp(m_ref[...] - m_new)
    p = jnp.exp(s - m_new)
    l_ref[...] = alpha * l_ref[...] + jnp.sum(p, axis=-1, keepdims=True)
    acc_ref[...] = alpha * acc_ref[...] + v_terms(p.astype(BF16))
    m_ref[...] = m_new


def _softmax_init(m_ref, l_ref, acc_ref):
    m_ref[...] = jnp.full(m_ref.shape, -jnp.inf, F32)
    l_ref[...] = jnp.zeros(l_ref.shape, F32)
    acc_ref[...] = jnp.zeros(acc_ref.shape, F32)


def _value_up_projection(acc_ref, l_ref, wuv_ref, y_ref, tq):
    for h in range(MLA_HEADS):
        rows = slice(h * tq, (h + 1) * tq)
        y = _dot_bf16(acc_ref[rows, :] / l_ref[rows, :], wuv_ref[h])
        y_ref[:, h * MLA_V_DIM:(h + 1) * MLA_V_DIM] = y.astype(y_ref.dtype)


PROMPT_ATTN_TILE = 256


def _mla_prompt_kernel(q_ref, k_ref, kt_ref, wuvt_ref, y_ref, m_ref, l_ref, acc_ref):
    tq = PROMPT_ATTN_TILE
    qi, ki = pl.program_id(1), pl.program_id(2)

    @pl.when(ki == 0)
    def _():
        _softmax_init(m_ref, l_ref, acc_ref)

    def step(on_diagonal):
        q = q_ref[...].reshape(MLA_HEADS * tq, MLA_QK)
        s = lax.dot_general(k_ref[...], q, NT_DIMS, preferred_element_type=F32) * MLA_SCALE
        if on_diagonal:
            k_pos = lax.broadcasted_iota(jnp.int32, s.shape, 0)
            q_pos = lax.broadcasted_iota(jnp.int32, s.shape, 1) & (tq - 1)
            s = jnp.where(k_pos <= q_pos, s, -jnp.inf)
        m_new = jnp.maximum(m_ref[...], jnp.max(s, axis=0, keepdims=True))
        alpha = jnp.exp(m_ref[...] - m_new)
        p = jnp.exp(s - m_new)
        l_ref[...] = alpha * l_ref[...] + jnp.sum(p, axis=0, keepdims=True)
        acc_ref[...] = alpha * acc_ref[...] + jnp.dot(kt_ref[...], p.astype(BF16),
                                                      preferred_element_type=F32)
        m_ref[...] = m_new

    pl.when(ki < qi)(lambda: step(False))
    pl.when(ki == qi)(lambda: step(True))

    @pl.when(ki == pl.num_programs(2) - 1)
    def _():
        o_t = (acc_ref[...] / l_ref[...]).astype(BF16)
        y_t = jnp.concatenate(
            [jnp.dot(wuvt_ref[h], o_t[:, h * tq:(h + 1) * tq], preferred_element_type=F32)
             for h in range(MLA_HEADS)], axis=0)
        y_ref[...] = y_t.T.astype(y_ref.dtype)


def mla_attend_prompt(q, k_full, k_lat_t, w_uv_t, batch):
    n = k_full.shape[0]
    t = PROMPT_ATTN_TILE
    n_t = n // batch // t
    rows = MLA_HEADS * t
    return pl.pallas_call(
        _mla_prompt_kernel,
        grid=(batch, n_t, n_t),
        in_specs=[pl.BlockSpec((MLA_HEADS, t, MLA_QK), lambda b, qi, ki: (0, b * n_t + qi, 0)),
                  pl.BlockSpec((t, MLA_QK), lambda b, qi, ki: (b * n_t + jnp.minimum(ki, qi), 0)),
                  pl.BlockSpec((MLA_KV_RANK, t), lambda b, qi, ki: (0, b * n_t + jnp.minimum(ki, qi))),
                  pl.BlockSpec(w_uv_t.shape, lambda b, qi, ki: (0, 0, 0))],
        out_specs=pl.BlockSpec((t, MLA_WIDTH), lambda b, qi, ki: (b * n_t + qi, 0)),
        out_shape=jax.ShapeDtypeStruct((n, MLA_WIDTH), BF16),
        scratch_shapes=[pltpu.VMEM((1, rows), F32), pltpu.VMEM((1, rows), F32),
                        pltpu.VMEM((MLA_KV_RANK, rows), F32)],
        compiler_params=_cparams(3),
        name="mla_attend_prompt",
    )(q, k_full, k_lat_t, w_uv_t)


PAGES_PER_STEP = 16


def _mla_sample_kernel(pt_ref, q_ref, *refs):
    del pt_ref
    pp = PAGES_PER_STEP
    lat_refs, rope_refs = refs[:pp], refs[pp:2 * pp]
    lat_new_ref, rope_new_ref, wuv_ref, y_ref, klat_ref, krope_t_ref, m_ref, l_ref, acc_ref = refs[2 * pp:]
    tq = q_ref.shape[1]
    j = pl.program_id(1)

    @pl.when(j == 0)
    def _():
        _softmax_init(m_ref, l_ref, acc_ref)

    q = q_ref[...].reshape(MLA_HEADS * tq, MLA_QK).astype(BF16)
    q_lat, q_rope = q[:, :MLA_KV_RANK], q[:, MLA_KV_RANK:MLA_KV_RANK + MLA_ROPE]

    for i in range(pp):
        span = slice(i * PAGE_SIZE, (i + 1) * PAGE_SIZE)
        klat_ref[span, :] = lat_refs[i][...].astype(BF16)
        krope_t_ref[:, span] = rope_refs[i][...].astype(BF16)
    s = (lax.dot_general(q_lat, klat_ref[...], NT_DIMS, preferred_element_type=F32)
         + jnp.dot(q_rope, krope_t_ref[...], preferred_element_type=F32)) * MLA_SCALE
    _softmax_update(s, lambda p: jnp.dot(p, klat_ref[...], preferred_element_type=F32), m_ref, l_ref, acc_ref)

    @pl.when(j == pl.num_programs(1) - 1)
    def _():
        pad = lambda x: jnp.concatenate([x, jnp.zeros((PAGE_SIZE - tq, x.shape[1]), F32)], axis=0).astype(BF16)
        lat_n = pad(lat_new_ref[...])
        s_n = (lax.dot_general(q_lat, lat_n, NT_DIMS, preferred_element_type=F32)
               + lax.dot_general(q_rope, pad(rope_new_ref[...]), NT_DIMS,
                                 preferred_element_type=F32)) * MLA_SCALE
        q_pos = lax.broadcasted_iota(jnp.int32, s_n.shape, 0) & (tq - 1)
        k_pos = lax.broadcasted_iota(jnp.int32, s_n.shape, 1)
        s_n = jnp.where(k_pos <= q_pos, s_n, -jnp.inf)
        _softmax_update(s_n, lambda p: jnp.dot(p, lat_n, preferred_element_type=F32), m_ref, l_ref, acc_ref)
        _value_up_projection(acc_ref, l_ref, wuv_ref, y_ref, tq)


def mla_attend_sample(q, lat_new, rope_new, cache_latent, cache_rope_t, page_table, layer, w_uv):
    bs, n_pages = page_table.shape
    tq = lat_new.shape[0] // bs
    pp = PAGES_PER_STEP
    page = lambda i: (lambda b, j, pt: (layer, pt[b, j * pp + i], 0, 0))
    in_specs = [pl.BlockSpec((MLA_HEADS, tq, MLA_QK), lambda b, j, pt: (0, b, 0))]
    in_specs += [pl.BlockSpec((None, None, PAGE_SIZE, MLA_KV_RANK), page(i)) for i in range(pp)]
    in_specs += [pl.BlockSpec((None, None, MLA_ROPE, PAGE_SIZE), page(i)) for i in range(pp)]
    in_specs += [pl.BlockSpec((tq, MLA_KV_RANK), lambda b, j, pt: (b, 0)),
                 pl.BlockSpec((tq, MLA_ROPE), lambda b, j, pt: (b, 0)),
                 pl.BlockSpec(w_uv.shape, lambda b, j, pt: (0, 0, 0))]
    rows = MLA_HEADS * tq
    return pl.pallas_call(
        _mla_sample_kernel,
        grid_spec=pltpu.PrefetchScalarGridSpec(
            num_scalar_prefetch=1,
            grid=(bs, n_pages // pp),
            in_specs=in_specs,
            out_specs=pl.BlockSpec((tq, MLA_WIDTH), lambda b, j, pt: (b, 0)),
            scratch_shapes=[pltpu.VMEM((pp * PAGE_SIZE, MLA_KV_RANK), BF16),
                            pltpu.VMEM((MLA_ROPE, pp * PAGE_SIZE), BF16),
                            pltpu.VMEM((rows, 1), F32), pltpu.VMEM((rows, 1), F32),
                            pltpu.VMEM((rows, MLA_KV_RANK), F32)]),
        out_shape=jax.ShapeDtypeStruct((bs * tq, MLA_WIDTH), F32),
        compiler_params=_cparams(2),
        name="mla_attend_sample",
    )(page_table, q, *([cache_latent] * pp), *([cache_rope_t] * pp), lat_new, rope_new, w_uv)


ROUTER_TILE = 256


def _top16_rows(s):
    iota = lax.broadcasted_iota(jnp.int32, s.shape, 0)
    rem = s
    vals = []
    for _ in range(PEER_TOPK):
        m = jnp.max(rem, axis=0, keepdims=True)
        idx = jnp.min(jnp.where(rem == m, iota, PEER_NKEYS), axis=0, keepdims=True)
        rem = jnp.where(iota == idx, -jnp.inf, rem)
        vals.append(m)
    return rem != s, vals


def _staircase_candidates(sv0, sv1):
    pairs = [(a, b) for a in range(PEER_TOPK) for b in range(PEER_TOPK) if (a + 1) * (b + 1) <= PEER_TOPK]
    rows = [sv0[a] + sv1[b] for a, b in pairs]
    rows += [jnp.full_like(rows[0], -jnp.inf)] * (-len(rows) % 8)
    return jnp.concatenate(rows, axis=0)


def _router_kernel(q_ref, keys_ref, s0_ref, s1_ref, e0_ref, e1_ref, thr_ref):
    half = PEER_DKEY // 2

    def per_head(h, carry):
        def scores(side):
            col = pl.multiple_of(h * PEER_DKEY + side * half, half)
            qh = q_ref[:, pl.ds(col, half)].astype(BF16)
            return lax.dot_general(keys_ref[h, side], qh, NT_DIMS, preferred_element_type=F32)

        s0, s1 = scores(0), scores(1)
        mask0, sv0 = _top16_rows(s0)
        mask1, sv1 = _top16_rows(s1)
        cand = _staircase_candidates(sv0, sv1)
        cmax = sv0[0] + sv1[0]
        rem = cand
        cum = jnp.zeros_like(cmax)
        thr = cmax
        for _ in range(PEER_TOPK):
            m = jnp.max(rem, axis=0, keepdims=True)
            eq = rem == m
            thr = jnp.where(cum < PEER_TOPK, m, thr)
            cum = cum + jnp.sum(eq.astype(F32), axis=0, keepdims=True)
            rem = jnp.where(eq, -jnp.inf, rem)
        z = jnp.sum(jnp.where(cand >= thr, jnp.exp(cand - cmax), 0.0), axis=0, keepdims=True)
        s0_ref[h] = s0
        s1_ref[h] = s1
        e0_ref[h] = jnp.where(mask0, jnp.exp(s0 - sv0[0]), 0.0)
        e1_ref[h] = jnp.where(mask1, jnp.exp(s1 - sv1[0]), 0.0) / z
        thr_ref[pl.ds(h, 1), :] = thr
        return carry

    lax.fori_loop(0, PEER_HEADS, per_head, 0)


def peer_router(q, keys_bf16):
    n = q.shape[0]
    t = ROUTER_TILE
    big = jax.ShapeDtypeStruct((PEER_HEADS, PEER_NKEYS, n), F32)
    big_spec = pl.BlockSpec((PEER_HEADS, PEER_NKEYS, t), lambda i: (0, 0, i))
    return pl.pallas_call(
        _router_kernel,
        grid=(n // t,),
        in_specs=[pl.BlockSpec((t, PEER_HEADS * PEER_DKEY), lambda i: (i, 0)),
                  pl.BlockSpec(keys_bf16.shape, lambda i: (0, 0, 0, 0))],
        out_specs=[big_spec, big_spec, big_spec, big_spec,
                   pl.BlockSpec((PEER_HEADS, t), lambda i: (0, i))],
        out_shape=[big, big, big, big, jax.ShapeDtypeStruct((PEER_HEADS, n), F32)],
        compiler_params=_cparams(1),
        name="peer_router",
    )(q, keys_bf16)


PEER_TOKEN_TILE = 512
PEER_EXPERT_TILE = 512


def _gelu_exact(a):
    return 0.5 * a * (1.0 + lax.erf(a * (2.0 ** -0.5)))


def _peer_dense_kernel(hn_ref, u_ref, v_ref, s0_ref, s1_ref, e0_ref, e1_ref, thr_ref, o_ref, coef_ref):
    e = pl.program_id(1)
    rows_per_tile = PEER_EXPERT_TILE // PEER_NKEYS
    last_tile = pl.num_programs(1) - 2

    @pl.when(e == 0)
    def _():
        o_ref[...] = jnp.zeros_like(o_ref)
        coef_ref[1] = jnp.zeros(coef_ref.shape[1:], BF16)

    o_ref[...] += lax.dot_general(coef_ref[(e + 1) % 2], v_ref[...], TN_DIMS,
                                  preferred_element_type=F32)

    act_t = lax.dot_general(u_ref[...], hn_ref[...], NT_DIMS, preferred_element_type=F32)
    coef = []
    for ib in range(rows_per_tile):
        i = jnp.minimum(e, last_tile) * rows_per_tile + ib
        gate = jnp.zeros((PEER_NKEYS, act_t.shape[1]), F32)
        for h in range(PEER_HEADS):
            picked = (s0_ref[h, pl.ds(i, 1), :] + s1_ref[h]) >= thr_ref[pl.ds(h, 1), :]
            gate = gate + jnp.where(picked, e0_ref[h, pl.ds(i, 1), :] * e1_ref[h], 0.0)
        a = act_t[ib * PEER_NKEYS:(ib + 1) * PEER_NKEYS]
        coef.append((gate * _gelu_exact(a)).astype(BF16))
    coef_ref[e % 2] = jnp.concatenate(coef, axis=0)


def peer_dense(hn, u_all, v_all, layer, s0, s1, e0, e1, thr):
    n, d = hn.shape
    tm, te = PEER_TOKEN_TILE, PEER_EXPERT_TILE
    n_e = PEER_EXPERTS // te
    once = pl.Buffered(1)
    big_spec = pl.BlockSpec((PEER_HEADS, PEER_NKEYS, tm), lambda i, e: (0, 0, i), pipeline_mode=once)
    return pl.pallas_call(
        _peer_dense_kernel,
        grid=(n // tm, n_e + 1),
        in_specs=[pl.BlockSpec((tm, d), lambda i, e: (i, 0), pipeline_mode=once),
                  pl.BlockSpec((None, te, d), lambda i, e: (layer, jnp.minimum(e, n_e - 1), 0)),
                  pl.BlockSpec((None, te, d), lambda i, e: (layer, jnp.maximum(e - 1, 0), 0)),
                  big_spec, big_spec, big_spec, big_spec,
                  pl.BlockSpec((PEER_HEADS, tm), lambda i, e: (0, i), pipeline_mode=once)],
        out_specs=pl.BlockSpec((tm, d), lambda i, e: (i, 0)),
        out_shape=jax.ShapeDtypeStruct((n, d), F32),
        scratch_shapes=[pltpu.VMEM((2, te, tm), BF16)],
        compiler_params=_cparams(2),
        name="peer_dense",
    )(hn, u_all, v_all, s0, s1, e0, e1, thr)


def _gated_residual_kernel(x_ref, gate_ref, y_ref, o_ref):
    o_ref[...] = x_ref[...] + gate_ref[...] * y_ref[...].reshape(x_ref.shape)


def gated_residual(x3, gate, y):
    gdim, r, d = x3.shape
    tg, rb = _token_tiling(x3)
    n_r = r // rb
    return pl.pallas_call(
        _gated_residual_kernel,
        grid=(gdim // tg, n_r),
        in_specs=[pl.BlockSpec((tg, rb, d), lambda gi, ri: (gi, ri, 0)),
                  pl.BlockSpec((tg, 1, d), lambda gi, ri: (gi, 0, 0)),
                  pl.BlockSpec((tg * rb, d), lambda gi, ri: (gi * n_r + ri, 0))],
        out_specs=pl.BlockSpec((tg, rb, d), lambda gi, ri: (gi, ri, 0)),
        out_shape=jax.ShapeDtypeStruct(x3.shape, F32),
        compiler_params=_cparams(2),
        name="gated_residual",
    )(x3, gate, y)


def _rmsnorm_kernel(x_ref, g_ref, o_ref):
    x = x_ref[...]
    o_ref[...] = x * lax.rsqrt(jnp.mean(x * x, axis=-1, keepdims=True) + EPS) * g_ref[...]


def rmsnorm_rows(x3, g):
    gdim, r, d = x3.shape
    tg, rb = _token_tiling(x3)
    return pl.pallas_call(
        _rmsnorm_kernel,
        grid=(gdim // tg, r // rb),
        in_specs=[pl.BlockSpec((tg, rb, d), lambda gi, ri: (gi, ri, 0)),
                  pl.BlockSpec((1, 1, d), lambda gi, ri: (0, 0, 0))],
        out_specs=pl.BlockSpec((tg, rb, d), lambda gi, ri: (gi, ri, 0)),
        out_shape=jax.ShapeDtypeStruct(x3.shape, F32),
        compiler_params=_cparams(2),
        name="final_rmsnorm",
    )(x3, g.reshape(1, 1, d))


def _decoder_layer(x3, mod, pool_hist, conv_hist, ssm_state, caches, start, p):
    b, l, d = x3.shape
    m = lambda i: mod[:, i, None, :]
    is_prompt = caches is None
    act_dtype = BF16 if is_prompt else F32
    u = norm_mod_matmul(x3, p['g_mix'], 1.0 + m(1), m(0), p['w_in'], tn=640, name="in_proj")
    y_pool, new_pool = pool_mixer(u, pool_hist, start, p['w_pool'], p['pool_scale'], act_dtype)
    y_ssd, new_conv, new_ssm = ssd_mixer(u, conv_hist, ssm_state, p['conv_w'], p['conv_b'], p['dt_bias'],
                                         p['a_log'], p['d_skip'], p['ssd_norm'], act_dtype,
                                         layer=None if is_prompt else p['layer'])
    q, k_full, k_lat_t, lat, krope = mla_prep(u, l, start, p['g_q'], p['g_kv'], p['w_uq'], p['w_uk'], act_dtype)
    if is_prompt:
        y_mla = mla_attend_prompt(q, k_full, k_lat_t, p['w_uv_t'], b)
    else:
        y_mla = mla_attend_sample(q, lat, krope, *caches, p['w_uv'])
    x3 = proj_residual(y_pool, y_ssd, y_mla, *p['w_out'], x3, m(2))
    q, hn = norm_mod_matmul(x3, p['g_ffn'], 1.0 + m(4), m(3), p['peer_wq'], tn=512, emit_hn=True, name="peer_query")
    s0, s1, e0, e1, thr = peer_router(q, p['peer_keys'])
    y = peer_dense(hn, p['peer_u'], p['peer_v'], p['layer'], s0, s1, e0, e1, thr)
    x3 = gated_residual(x3, m(5), y)
    return (x3, new_pool, new_conv, new_ssm,
            lat.reshape(b, l, MLA_KV_RANK), krope.reshape(b, l, MLA_ROPE))


def _pack_w_in(w):
    d = w.shape[0]
    sizes = (POOL_WIDTH, SSD_WIDTH, SSD_CONV_DIM, SSD_HEADS, MLA_Q_RANK, MLA_KV_RANK, MLA_ROPE)
    offs = [0]
    for s in sizes:
        offs.append(offs[-1] + s)
    pool, z, xbc, dt, cq, ckv, kr = (w[:, offs[i]:offs[i + 1]] for i in range(len(sizes)))
    zeros = lambda k: jnp.zeros((d, k), w.dtype)
    packed = jnp.concatenate([z, xbc, pool, cq, ckv, kr, zeros(LANES - MLA_ROPE),
                              dt, zeros(LANES - SSD_HEADS)], axis=1)
    assert packed.shape[1] == D_IN_PACKED
    return packed.astype(BF16)


def _pack_w_uq(w):
    r = w.shape[0]
    padded = jnp.concatenate([w, jnp.zeros((r, MLA_HEADS, MLA_QHEAD - w.shape[2]), w.dtype)], axis=-1)
    return padded.reshape(r, MLA_HEADS * MLA_QHEAD).astype(BF16)


def kernel(x_prompt, x_sample, cache_latent, cache_rope, state_ssm, state_conv, state_pool, page_table, c_prompt, c_sample, w_ada, b_ada, ada_table, g_mix, w_in, w_pool, pool_scale, conv_w, conv_b, dt_bias, a_log, d_skip, ssd_norm, g_q, w_uq, g_kv, w_uk, w_uv, w_out, g_ffn, peer_wq, peer_keys, peer_u, peer_v, g_final):
    depth = w_in.shape[0]
    bp = x_prompt.shape[0]
    bs = x_sample.shape[0]
    past_len = page_table.shape[1] * PAGE_SIZE

    n_c = bp + bs
    c_all = jnp.concatenate([c_prompt, c_sample, jnp.zeros((-n_c % 8, D_MODEL), F32)], axis=0)
    mod_all = adaln_base(c_all, w_ada, b_ada)
    mod_p = mod_all[:bp].reshape(bp, N_MOD, D_MODEL)
    mod_s = mod_all[bp:n_c].reshape(bs, N_MOD, D_MODEL)

    zero_pool = jnp.zeros((bp, POOL_HIST, POOL_WIDTH), F32)
    zero_conv = jnp.zeros((bp, SSD_CONV - 1, SSD_CONV_DIM), F32)
    zero_ssm = jnp.zeros((bp, SSD_HEADS, SSD_HEAD_DIM, SSD_STATE), F32)

    peer_u_bf16, peer_v_bf16 = peer_u.astype(BF16), peer_v.astype(BF16)
    cache_rope_t = jnp.swapaxes(cache_rope, 2, 3)

    hp, hs = x_prompt, x_sample
    outs = [[] for _ in range(10)]
    for l in range(depth):
        w_out_l = w_out[l].astype(BF16)
        p = {'g_mix': g_mix[l], 'w_in': _pack_w_in(w_in[l]), 'w_pool': w_pool[l].astype(BF16),
             'pool_scale': pool_scale[l], 'conv_w': conv_w[l], 'conv_b': conv_b[l], 'dt_bias': dt_bias[l],
             'a_log': a_log[l], 'd_skip': d_skip[l], 'ssd_norm': ssd_norm[l], 'g_q': g_q[l],
             'w_uq': _pack_w_uq(w_uq[l]), 'g_kv': g_kv[l],
             'w_uk': jnp.transpose(w_uk[l], (1, 2, 0)).astype(BF16),
             'w_uv': jnp.transpose(w_uv[l], (1, 0, 2)).astype(BF16),
             'w_uv_t': jnp.transpose(w_uv[l], (1, 2, 0)).astype(BF16),
             'w_out': (w_out_l[:POOL_WIDTH], w_out_l[POOL_WIDTH:POOL_WIDTH + SSD_WIDTH],
                       w_out_l[POOL_WIDTH + SSD_WIDTH:]),
             'g_ffn': g_ffn[l], 'peer_wq': peer_wq[l].astype(BF16), 'peer_keys': peer_keys[l].astype(BF16),
             'peer_u': peer_u_bf16, 'peer_v': peer_v_bf16, 'layer': l}
        hp, np_pool, np_conv, np_ssm, np_lat, np_rope = _decoder_layer(
            hp, mod_p + ada_table[l], zero_pool, zero_conv, zero_ssm, None, 0, p)
        hs, ns_pool, ns_conv, ns_ssm, ns_lat, ns_rope = _decoder_layer(
            hs, mod_s + ada_table[l], state_pool[l], state_conv, state_ssm,
            (cache_latent, cache_rope_t, page_table, l), past_len, p)
        for lst, val in zip(outs, (np_lat, np_rope, ns_lat, ns_rope, np_ssm, ns_ssm, np_conv, ns_conv,
                                   np_pool, ns_pool)):
            lst.append(val)
    y_prompt = rmsnorm_rows(hp, g_final)
    y_sample = rmsnorm_rows(hs, g_final)
    return (y_prompt, y_sample) + tuple(jnp.stack(o) for o in outs)
```

```python
import functools

import jax
import jax.numpy as jnp
from jax import lax
from jax.experimental import pallas as pl
from jax.experimental.pallas import tpu as pltpu

F32 = jnp.float32
BF16 = jnp.bfloat16
HIGHEST = lax.Precision.HIGHEST

D_MODEL = 4096
N_MOD = 6
EPS = 1e-6
PAGE_SIZE = 128

POOL_WIDTH = D_MODEL // 4
POOL_WINDOWS = (2, 4, 8, 16)
POOL_GROUP = POOL_WIDTH // len(POOL_WINDOWS)
POOL_HIST = max(POOL_WINDOWS) - 1

SSD_WIDTH = 3 * D_MODEL // 8
SSD_HEAD_DIM = 64
SSD_HEADS = SSD_WIDTH // SSD_HEAD_DIM
SSD_GROUPS = 4
SSD_GROUP_HEADS = SSD_HEADS // SSD_GROUPS
SSD_GROUP_WIDTH = SSD_WIDTH // SSD_GROUPS
SSD_STATE = 128
SSD_CONV = 4
SSD_CHUNK = 128
SSD_CONV_DIM = SSD_WIDTH + 2 * SSD_GROUPS * SSD_STATE

MLA_WIDTH = D_MODEL - POOL_WIDTH - SSD_WIDTH
MLA_V_DIM = 128
MLA_HEADS = MLA_WIDTH // MLA_V_DIM
MLA_NOPE = 128
MLA_ROPE = 64
MLA_Q_RANK = 3 * D_MODEL // 16
MLA_KV_RANK = D_MODEL // 16
ROPE_THETA = 10000.0

PEER_HEADS = 8
PEER_NKEYS = 128
PEER_EXPERTS = PEER_NKEYS * PEER_NKEYS
PEER_TOPK = 16
PEER_DKEY = 256

LANES = 128
OFF_Z = 0
OFF_XBC = OFF_Z + SSD_WIDTH
OFF_POOL = OFF_XBC + SSD_CONV_DIM
OFF_CQ = OFF_POOL + POOL_WIDTH
OFF_CKV = OFF_CQ + MLA_Q_RANK
OFF_KR = OFF_CKV + MLA_KV_RANK
OFF_DT = OFF_KR + LANES
D_IN_PACKED = OFF_DT + LANES
ZX_WIDTH = SSD_WIDTH + SSD_CONV_DIM
MLA_SLAB = D_IN_PACKED - OFF_CQ
MLA_QK = MLA_KV_RANK + LANES
MLA_QHEAD = MLA_NOPE + LANES

VMEM_LIMIT = 56 * 1024 * 1024
ROW_TILE = 512

NT_DIMS = (((1,), (1,)), ((), ()))
TN_DIMS = (((0,), (0,)), ((), ()))


def _cparams(n_axes):
    return pltpu.CompilerParams(dimension_semantics=("arbitrary",) * n_axes,
                                vmem_limit_bytes=VMEM_LIMIT)


def _token_tiling(x3):
    g, r, _ = x3.shape
    if r >= ROW_TILE:
        assert r % ROW_TILE == 0
        return 1, ROW_TILE
    tg = min(ROW_TILE // r, 32)
    assert ROW_TILE % r == 0 and g % tg == 0
    return tg, r


def _silu(x):
    return x * jax.nn.sigmoid(x)


BF16_ROWS = 16


def _dot_bf16(a, w):
    m = a.shape[0]
    if m < BF16_ROWS:
        a = jnp.concatenate([a, jnp.zeros((BF16_ROWS - m, a.shape[1]), a.dtype)], axis=0)
    return jnp.dot(a.astype(BF16), w, preferred_element_type=F32)[:m]


def _adaln_kernel(c_ref, w_ref, b_ref, o_ref):
    a = _silu(c_ref[...]).astype(BF16)
    o_ref[...] = jnp.dot(a, w_ref[...].astype(BF16), preferred_element_type=F32) + b_ref[...]


def adaln_base(c, w_ada, b_ada):
    n, d = c.shape
    dout = w_ada.shape[1]
    tn = 512
    return pl.pallas_call(
        _adaln_kernel,
        grid=(dout // tn,),
        in_specs=[pl.BlockSpec((n, d), lambda j: (0, 0)),
                  pl.BlockSpec((d, tn), lambda j: (0, j)),
                  pl.BlockSpec((1, tn), lambda j: (0, j))],
        out_specs=pl.BlockSpec((n, tn), lambda j: (0, j)),
        out_shape=jax.ShapeDtypeStruct((n, dout), F32),
        compiler_params=_cparams(1),
        name="adaln",
    )(c, w_ada, b_ada.reshape(1, dout))


def _norm_mod_matmul_kernel(x_ref, g_ref, sc_ref, sh_ref, w_ref, o_ref, *rest, emit_hn):
    if emit_hn:
        hn_out_ref, hn_s = rest
    else:
        (hn_s,) = rest

    @pl.when(pl.program_id(2) == 0)
    def _():
        x = x_ref[...]
        y = x * lax.rsqrt(jnp.mean(x * x, axis=-1, keepdims=True) + EPS) * g_ref[...]
        hn = (y * sc_ref[...] + sh_ref[...]).reshape(hn_s.shape).astype(BF16)
        hn_s[...] = hn
        if emit_hn:
            hn_out_ref[...] = hn

    o_ref[...] = jnp.dot(hn_s[...], w_ref[...], preferred_element_type=F32)


def norm_mod_matmul(x3, g, scale1p, shift, w, tn, emit_hn=False, name="norm_mod_matmul"):
    gdim, r, d = x3.shape
    dout = w.shape[1]
    tg, rb = _token_tiling(x3)
    tm = tg * rb
    n_r = r // rb
    grid = (gdim // tg, n_r, dout // tn)
    out_shape = [jax.ShapeDtypeStruct((gdim * r, dout), F32)]
    out_specs = [pl.BlockSpec((tm, tn), lambda gi, ri, j: (gi * n_r + ri, j))]
    if emit_hn:
        out_shape.append(jax.ShapeDtypeStruct((gdim * r, d), BF16))
        out_specs.append(pl.BlockSpec((tm, d), lambda gi, ri, j: (gi * n_r + ri, 0)))
    res = pl.pallas_call(
        functools.partial(_norm_mod_matmul_kernel, emit_hn=emit_hn),
        grid=grid,
        in_specs=[pl.BlockSpec((tg, rb, d), lambda gi, ri, j: (gi, ri, 0)),
                  pl.BlockSpec((1, 1, d), lambda gi, ri, j: (0, 0, 0)),
                  pl.BlockSpec((tg, 1, d), lambda gi, ri, j: (gi, 0, 0)),
                  pl.BlockSpec((tg, 1, d), lambda gi, ri, j: (gi, 0, 0)),
                  pl.BlockSpec((d, tn), lambda gi, ri, j: (0, j))],
        out_specs=out_specs,
        out_shape=out_shape,
        scratch_shapes=[pltpu.VMEM((tm, d), BF16)],
        compiler_params=_cparams(3),
        name=name,
    )(x3, g.reshape(1, 1, d), scale1p, shift, w)
    return res if emit_hn else res[0]


def _proj_residual_kernel(yp_ref, ys_ref, ym_ref, w1_ref, w2_ref, w3_ref, x_ref, gate_ref, o_ref):
    acc = jnp.dot(yp_ref[...].astype(BF16), w1_ref[...], preferred_element_type=F32)
    acc += jnp.dot(ys_ref[...].astype(BF16), w2_ref[...], preferred_element_type=F32)
    acc += jnp.dot(ym_ref[...].astype(BF16), w3_ref[...], preferred_element_type=F32)
    o_ref[...] = x_ref[...] + gate_ref[...] * acc.reshape(x_ref.shape)


def proj_residual(y_pool, y_ssd, y_mla, w1, w2, w3, x3, gate):
    gdim, r, d = x3.shape
    tg, rb = _token_tiling(x3)
    tm = tg * rb
    n_r = r // rb
    tn = 512
    row = lambda gi, ri, j: (gi * n_r + ri, 0)
    return pl.pallas_call(
        _proj_residual_kernel,
        grid=(gdim // tg, n_r, d // tn),
        in_specs=[pl.BlockSpec((tm, y_pool.shape[1]), row),
                  pl.BlockSpec((tm, y_ssd.shape[1]), row),
                  pl.BlockSpec((tm, y_mla.shape[1]), row),
                  pl.BlockSpec((w1.shape[0], tn), lambda gi, ri, j: (0, j)),
                  pl.BlockSpec((w2.shape[0], tn), lambda gi, ri, j: (0, j)),
                  pl.BlockSpec((w3.shape[0], tn), lambda gi, ri, j: (0, j)),
                  pl.BlockSpec((tg, rb, tn), lambda gi, ri, j: (gi, ri, j)),
                  pl.BlockSpec((tg, 1, tn), lambda gi, ri, j: (gi, 0, j))],
        out_specs=pl.BlockSpec((tg, rb, tn), lambda gi, ri, j: (gi, ri, j)),
        out_shape=jax.ShapeDtypeStruct(x3.shape, F32),
        compiler_params=_cparams(3),
        name="proj_residual",
    )(y_pool, y_ssd, y_mla, w1, w2, w3, x3, gate)


POOL_PAD = POOL_HIST + 1


def _pool_kernel(u_ref, hist_ref, w_ref, scale_ref, y_ref, newhist_ref, ext_ref, *, start, tl):
    t = pl.program_id(1)

    @pl.when(t == 0)
    def _():
        ext_ref[0:1, :] = jnp.zeros((1, POOL_WIDTH), F32)
        ext_ref[1:POOL_PAD, :] = hist_ref[0]

    ext_ref[POOL_PAD:POOL_PAD + tl, :] = u_ref[...]
    pos = start + t * tl + lax.broadcasted_iota(jnp.int32, (tl, POOL_GROUP), 0)
    for g, w in enumerate(POOL_WINDOWS):
        c0 = g * POOL_GROUP
        cur = ext_ref[POOL_PAD:POOL_PAD + tl, c0:c0 + POOL_GROUP]
        acc = cur
        for k in range(1, w):
            acc = acc + ext_ref[POOL_PAD - k:POOL_PAD - k + tl, c0:c0 + POOL_GROUP]
        cnt = jnp.minimum(pos + 1, w).astype(F32)
        d = acc / cnt - cur
        y = _dot_bf16(d, w_ref[g]) * scale_ref[:, c0:c0 + POOL_GROUP]
        y_ref[:, c0:c0 + POOL_GROUP] = y.astype(y_ref.dtype)
    tail = ext_ref[tl + 1:tl + POOL_PAD, :]
    newhist_ref[0] = tail
    ext_ref[1:POOL_PAD, :] = tail


def pool_mixer(u, hist, start, w_pool, pool_scale, out_dtype):
    b = hist.shape[0]
    l = u.shape[0] // b
    tl = min(l, 256)
    n_t = l // tl
    return pl.pallas_call(
        functools.partial(_pool_kernel, start=start, tl=tl),
        grid=(b, n_t),
        in_specs=[pl.BlockSpec((tl, POOL_WIDTH), lambda bi, t: (bi * n_t + t, OFF_POOL // POOL_WIDTH)),
                  pl.BlockSpec((1, POOL_HIST, POOL_WIDTH), lambda bi, t: (bi, 0, 0)),
                  pl.BlockSpec(w_pool.shape, lambda bi, t: (0, 0, 0)),
                  pl.BlockSpec((1, POOL_WIDTH), lambda bi, t: (0, 0))],
        out_specs=[pl.BlockSpec((tl, POOL_WIDTH), lambda bi, t: (bi * n_t + t, 0)),
                   pl.BlockSpec((1, POOL_HIST, POOL_WIDTH), lambda bi, t: (bi, 0, 0))],
        out_shape=[jax.ShapeDtypeStruct((b * l, POOL_WIDTH), out_dtype),
                   jax.ShapeDtypeStruct((b, POOL_HIST, POOL_WIDTH), F32)],
        scratch_shapes=[pltpu.VMEM((POOL_PAD + tl, POOL_WIDTH), F32)],
        compiler_params=_cparams(2),
        name="pool_mixer",
    )(u, hist, w_pool, pool_scale.reshape(1, POOL_WIDTH))


CONV_PAD = 8


def _ssd_conv_silu(ext_ref, xbc, hist, convw_ref, convb_ref, q):
    if hist is not None:
        ext_ref[CONV_PAD - 3:CONV_PAD, :] = hist
    ext_ref[CONV_PAD:CONV_PAD + q, :] = xbc
    conv = convb_ref[...]
    for k in range(SSD_CONV):
        lo = CONV_PAD - (SSD_CONV - 1) + k
        conv = conv + convw_ref[k:k + 1, :] * ext_ref[lo:lo + q, :]
    tail = ext_ref[CONV_PAD + q - 3:CONV_PAD + q, :]
    return _silu(conv), tail


def _softplus(x):
    return jnp.maximum(x, 0.0) + jnp.log(1.0 + jnp.exp(-jnp.abs(x)))


def _ssd_gate_norm(y, z, normg_ref, y_ref):
    y = y * _silu(z)
    for g in range(SSD_GROUPS):
        seg = y[:, g * SSD_GROUP_WIDTH:(g + 1) * SSD_GROUP_WIDTH]
        seg = seg * lax.rsqrt(jnp.mean(seg * seg, axis=-1, keepdims=True) + EPS)
        seg = seg * normg_ref[:, g * SSD_GROUP_WIDTH:(g + 1) * SSD_GROUP_WIDTH]
        y_ref[:, g * SSD_GROUP_WIDTH:(g + 1) * SSD_GROUP_WIDTH] = seg.astype(y_ref.dtype)


def _ssd_chunk_kernel(zx_ref, dt_ref, convh_ref, ssm_ref, convw_ref, convb_ref, dtb_ref, alog_ref,
                      dskip_ref, normg_ref, y_ref, newconv_ref, newssm_ref, ext_ref, state_ref, ybuf_ref):
    q = SSD_CHUNK
    c = pl.program_id(1)

    @pl.when(c == 0)
    def _():
        ext_ref[CONV_PAD - 3:CONV_PAD, :] = convh_ref[0]
        state_ref[...] = ssm_ref[0].reshape(state_ref.shape)

    xa, tail = _ssd_conv_silu(ext_ref, zx_ref[:, SSD_WIDTH:], None, convw_ref, convb_ref, q)
    newconv_ref[0] = tail
    ext_ref[CONV_PAD - 3:CONV_PAD, :] = tail
    xs = xa[:, :SSD_WIDTH]

    dt = _softplus(dt_ref[...] + dtb_ref[...])
    a = dt * -jnp.exp(alog_ref[...])
    li = lax.broadcasted_iota(jnp.int32, (q, q), 0)
    si = lax.broadcasted_iota(jnp.int32, (q, q), 1)
    tril = si <= li
    a_cs = jnp.dot(tril.astype(F32), a, precision=HIGHEST, preferred_element_type=F32)
    a_cs_t = lax.dot_general((li == si).astype(F32), a_cs, NT_DIMS, precision=HIGHEST,
                             preferred_element_type=F32)
    a_last = a_cs[q - 1:q, :]

    for g in range(SSD_GROUPS):
        b0 = SSD_WIDTH + g * SSD_STATE
        c0 = SSD_WIDTH + SSD_GROUPS * SSD_STATE + g * SSD_STATE
        bg = xa[:, b0:b0 + SSD_STATE].astype(BF16)
        cg = xa[:, c0:c0 + SSD_STATE].astype(BF16)
        cb = lax.dot_general(cg, bg, NT_DIMS, preferred_element_type=F32)
        r0 = g * SSD_GROUP_WIDTH
        y_off = lax.dot_general(cg, state_ref[r0:r0 + SSD_GROUP_WIDTH, :].astype(BF16), NT_DIMS,
                                preferred_element_type=F32)
        xdd = []
        for r in range(SSD_GROUP_HEADS):
            h = g * SSD_GROUP_HEADS + r
            col = a_cs[:, h:h + 1]
            xd = xs[:, h * SSD_HEAD_DIM:(h + 1) * SSD_HEAD_DIM] * dt[:, h:h + 1]
            decay = jnp.exp(jnp.where(tril, col - a_cs_t[h:h + 1, :], -jnp.inf))
            y_diag = jnp.dot((cb * decay).astype(BF16), xd.astype(BF16), preferred_element_type=F32)
            ybuf_ref[:, h * SSD_HEAD_DIM:(h + 1) * SSD_HEAD_DIM] = (
                y_diag + y_off[:, r * SSD_HEAD_DIM:(r + 1) * SSD_HEAD_DIM] * jnp.exp(col))
            xdd.append(xd * jnp.exp(a_last[:, h:h + 1] - col))
        new = lax.dot_general(jnp.concatenate(xdd, axis=1).astype(BF16), bg, TN_DIMS,
                              preferred_element_type=F32)
        for r in range(SSD_GROUP_HEADS):
            h = g * SSD_GROUP_HEADS + r
            rows = slice(h * SSD_HEAD_DIM, (h + 1) * SSD_HEAD_DIM)
            state_ref[rows, :] = (state_ref[rows, :] * jnp.exp(a_last[:, h:h + 1])
                                  + new[r * SSD_HEAD_DIM:(r + 1) * SSD_HEAD_DIM])

    y = ybuf_ref[...] + dskip_ref[...] * xs
    _ssd_gate_norm(y, zx_ref[:, :SSD_WIDTH], normg_ref, y_ref)

    @pl.when(c == pl.num_programs(1) - 1)
    def _():
        newssm_ref[0] = state_ref[...].reshape(newssm_ref.shape[1:])


def _ssd_step_kernel(zx_ref, dt_ref, convh_ref, ssm_ref, convw_ref, convb_ref, dtb_ref, alog_ref,
                     dskip_ref, normg_ref, y_ref, newconv_ref, newssm_ref, ext_ref):
    q = zx_ref.shape[0]
    xa, tail = _ssd_conv_silu(ext_ref, zx_ref[:, SSD_WIDTH:], convh_ref[0], convw_ref, convb_ref, q)
    newconv_ref[0] = tail
    xs = xa[:, :SSD_WIDTH]

    dt = _softplus(dt_ref[...] + dtb_ref[...])
    a = dt * -jnp.exp(alog_ref[...])
    rows128 = lax.broadcasted_iota(jnp.int32, (q, LANES), 0)
    a_cs = jnp.zeros((q, LANES), F32)
    for s in range(q):
        a_cs = a_cs + jnp.where(rows128 >= s, a[s:s + 1, :], 0.0)
    a_last = a_cs[q - 1:q, :]
    state = ssm_ref[0].reshape(SSD_WIDTH, SSD_STATE)
    pad_rows = SSD_CHUNK - q
    gn = SSD_GROUPS * SSD_STATE
    bm = xa[:, SSD_WIDTH:SSD_WIDTH + gn]
    cm = xa[:, SSD_WIDTH + gn:]

    group_of_lane = lax.broadcasted_iota(jnp.int32, (gn, LANES), 0) // SSD_STATE
    head = lax.broadcasted_iota(jnp.int32, (gn, LANES), 1)
    group_sum = (head // SSD_GROUP_HEADS == group_of_lane).astype(F32)
    spread = (lax.broadcasted_iota(jnp.int32, (LANES, SSD_WIDTH), 1) // SSD_HEAD_DIM
              == lax.broadcasted_iota(jnp.int32, (LANES, SSD_WIDTH), 0)).astype(F32)
    cb = jnp.dot(jnp.concatenate([cm * bm[s:s + 1, :] for s in range(q)], axis=0), group_sum,
                 precision=HIGHEST, preferred_element_type=F32)
    weights = [jnp.where(rows128 >= s, jnp.exp(a_cs - a_cs[s:s + 1, :]), 0.0) * cb[s * q:(s + 1) * q]
               for s in range(q)]
    per_head = jnp.concatenate([dt, jnp.exp(a_cs), jnp.exp(a_last - a_cs)] + weights, axis=0)
    wide = jnp.dot(per_head, spread, precision=HIGHEST, preferred_element_type=F32)
    xd = xs * wide[0:q]
    y = dskip_ref[...] * xs
    for s in range(q):
        y = y + wide[(3 + s) * q:(4 + s) * q] * xd[s:s + 1, :]
    xdd = xd * wide[2 * q:3 * q]

    y_off = []
    for g in range(SSD_GROUPS):
        r0 = g * SSD_GROUP_WIDTH
        old = state[r0:r0 + SSD_GROUP_WIDTH, :]
        cg = cm[:, g * SSD_STATE:(g + 1) * SSD_STATE]
        cg16 = jnp.concatenate([cg, jnp.zeros((BF16_ROWS - q, SSD_STATE), F32)], axis=0).astype(BF16)
        y_off.append(lax.dot_general(cg16, old.astype(BF16), NT_DIMS, preferred_element_type=F32)[:q])
        xdd_p = jnp.concatenate([xdd[:, r0:r0 + SSD_GROUP_WIDTH],
                                 jnp.zeros((pad_rows, SSD_GROUP_WIDTH), F32)], axis=0).astype(BF16)
        bg_p = jnp.concatenate([bm[:, g * SSD_STATE:(g + 1) * SSD_STATE],
                                jnp.zeros((pad_rows, SSD_STATE), F32)], axis=0).astype(BF16)
        new = lax.dot_general(xdd_p, bg_p, TN_DIMS, preferred_element_type=F32)
        for r in range(SSD_GROUP_HEADS):
            h = g * SSD_GROUP_HEADS + r
            newssm_ref[0, h] = (old[r * SSD_HEAD_DIM:(r + 1) * SSD_HEAD_DIM] * jnp.exp(a_last[:, h:h + 1])
                                + new[r * SSD_HEAD_DIM:(r + 1) * SSD_HEAD_DIM])

    y = y + jnp.concatenate(y_off, axis=1) * wide[q:2 * q]
    _ssd_gate_norm(y, zx_ref[:, :SSD_WIDTH], normg_ref, y_ref)


def ssd_mixer(u, conv_hist, ssm_state, conv_w, conv_b, dt_bias, a_log, d_skip, norm_g, out_dtype, layer=None):
    conv_shape, ssm_shape = conv_hist.shape[-3:], ssm_state.shape[-4:]
    b = conv_shape[0]
    l = u.shape[0] // b
    if layer is None:
        conv_spec = pl.BlockSpec((1,) + conv_shape[1:], lambda bi, c: (bi, 0, 0))
        ssm_spec = pl.BlockSpec((1,) + ssm_shape[1:], lambda bi, c: (bi, 0, 0, 0))
    else:
        conv_spec = pl.BlockSpec((None, 1) + conv_shape[1:], lambda bi, c: (layer, bi, 0, 0))
        ssm_spec = pl.BlockSpec((None, 1) + ssm_shape[1:], lambda bi, c: (layer, bi, 0, 0, 0))
    pad = lambda v: jnp.pad(v, (0, LANES - SSD_HEADS)).reshape(1, LANES)
    params = (conv_w, conv_b.reshape(1, SSD_CONV_DIM), pad(dt_bias), pad(a_log),
              jnp.repeat(d_skip, SSD_HEAD_DIM).reshape(1, SSD_WIDTH), norm_g.reshape(1, SSD_WIDTH))
    chunked = l % SSD_CHUNK == 0
    q = SSD_CHUNK if chunked else l
    n_c = l // q
    const2 = lambda bi, c: (0, 0)
    in_specs = [pl.BlockSpec((q, ZX_WIDTH), lambda bi, c: (bi * n_c + c, 0)),
                pl.BlockSpec((q, LANES), lambda bi, c: (bi * n_c + c, OFF_DT // LANES)),
                conv_spec, ssm_spec]
    in_specs += [pl.BlockSpec(p.shape, const2) for p in params]
    scratch = [pltpu.VMEM((CONV_PAD + q, SSD_CONV_DIM), F32)]
    if chunked:
        scratch += [pltpu.VMEM((SSD_WIDTH, SSD_STATE), F32), pltpu.VMEM((q, SSD_WIDTH), F32)]
    return pl.pallas_call(
        _ssd_chunk_kernel if chunked else _ssd_step_kernel,
        grid=(b, n_c),
        in_specs=in_specs,
        out_specs=[pl.BlockSpec((q, SSD_WIDTH), lambda bi, c: (bi * n_c + c, 0)),
                   pl.BlockSpec((1,) + conv_shape[1:], lambda bi, c: (bi, 0, 0)),
                   pl.BlockSpec((1,) + ssm_shape[1:], lambda bi, c: (bi, 0, 0, 0))],
        out_shape=[jax.ShapeDtypeStruct((b * l, SSD_WIDTH), out_dtype),
                   jax.ShapeDtypeStruct(conv_shape, F32),
                   jax.ShapeDtypeStruct(ssm_shape, F32)],
        scratch_shapes=scratch,
        compiler_params=_cparams(2),
        name="ssd_chunked" if chunked else "ssd_step",
    )(u, u, conv_hist, ssm_state, *params)


def _rotate(slab, rope_ref):
    return (slab * rope_ref[0]
            + pltpu.roll(slab, LANES - MLA_ROPE // 2, 1) * rope_ref[1]
            + pltpu.roll(slab, MLA_ROPE // 2, 1) * rope_ref[2])


def _mla_prep_kernel(slab_ref, rope_ref, gq_ref, gkv_ref, wuq_ref, wuk_ref,
                     q_ref, kfull_ref, lat_t_ref, lat_ref, krope_ref):
    def norm(x, g_ref):
        return x * lax.rsqrt(jnp.mean(x * x, axis=-1, keepdims=True) + EPS) * g_ref[...]

    qn = norm(slab_ref[:, :MLA_Q_RANK], gq_ref).astype(BF16)
    lat = norm(slab_ref[:, MLA_Q_RANK:MLA_Q_RANK + MLA_KV_RANK], gkv_ref)
    k_rot = _rotate(slab_ref[:, OFF_KR - OFF_CQ:OFF_KR - OFF_CQ + LANES], rope_ref)
    lat_ref[...] = lat
    krope_ref[...] = k_rot[:, :MLA_ROPE]
    kfull_ref[...] = jnp.concatenate([lat, k_rot], axis=-1).astype(BF16)
    lat_t_ref[...] = lat.T.astype(BF16)
    for h in range(MLA_HEADS):
        qh = jnp.dot(qn, wuq_ref[:, h * MLA_QHEAD:(h + 1) * MLA_QHEAD], preferred_element_type=F32)
        q_lat = jnp.dot(qh[:, :MLA_NOPE].astype(BF16), wuk_ref[h], preferred_element_type=F32)
        q_ref[h] = jnp.concatenate([q_lat, _rotate(qh[:, MLA_NOPE:], rope_ref)], axis=-1).astype(q_ref.dtype)


def _rope_tables(pos):
    half = MLA_ROPE // 2
    inv_freq = ROPE_THETA ** (-jnp.arange(half, dtype=F32) / half)
    ang = pos.astype(F32)[:, None] * inv_freq[None, :]
    cos, sin = jnp.cos(ang), jnp.sin(ang)
    z = jnp.zeros_like(cos)
    return jnp.stack([jnp.concatenate([cos, cos, z, z], axis=-1),
                      jnp.concatenate([-sin, z, z, z], axis=-1),
                      jnp.concatenate([z, sin, z, z], axis=-1)])


def mla_prep(u, seq_len, start, g_q, g_kv, w_uq_packed, w_uk_t, q_dtype):
    n = u.shape[0]
    tl = ROW_TILE
    reps = max(tl // seq_len, 1)
    rope = _rope_tables(jnp.tile(start + jnp.arange(seq_len), reps))
    n_rope = rope.shape[1] // tl
    return pl.pallas_call(
        _mla_prep_kernel,
        grid=(n // tl,),
        in_specs=[pl.BlockSpec((tl, MLA_SLAB), lambda i: (i, OFF_CQ // MLA_SLAB)),
                  pl.BlockSpec((3, tl, LANES), lambda i: (0, i % n_rope, 0)),
                  pl.BlockSpec((1, MLA_Q_RANK), lambda i: (0, 0)),
                  pl.BlockSpec((1, MLA_KV_RANK), lambda i: (0, 0)),
                  pl.BlockSpec(w_uq_packed.shape, lambda i: (0, 0)),
                  pl.BlockSpec(w_uk_t.shape, lambda i: (0, 0, 0))],
        out_specs=[pl.BlockSpec((MLA_HEADS, tl, MLA_QK), lambda i: (0, i, 0)),
                   pl.BlockSpec((tl, MLA_QK), lambda i: (i, 0)),
                   pl.BlockSpec((MLA_KV_RANK, tl), lambda i: (0, i)),
                   pl.BlockSpec((tl, MLA_KV_RANK), lambda i: (i, 0)),
                   pl.BlockSpec((tl, MLA_ROPE), lambda i: (i, 0))],
        out_shape=[jax.ShapeDtypeStruct((MLA_HEADS, n, MLA_QK), q_dtype),
                   jax.ShapeDtypeStruct((n, MLA_QK), BF16),
                   jax.ShapeDtypeStruct((MLA_KV_RANK, n), BF16),
                   jax.ShapeDtypeStruct((n, MLA_KV_RANK), F32),
                   jax.ShapeDtypeStruct((n, MLA_ROPE), F32)],
        compiler_params=_cparams(1),
        name="mla_prep",
    )(u, rope, g_q.reshape(1, MLA_Q_RANK), g_kv.reshape(1, MLA_KV_RANK), w_uq_packed, w_uk_t)


MLA_SCALE = (MLA_NOPE + MLA_ROPE) ** -0.5


def _softmax_update(s, v_terms, m_ref, l_ref, acc_ref):
    m_new = jnp.maximum(m_ref[...], jnp.max(s, axis=-1, keepdims=True))
    alpha = jnp.exp(m_ref[...] - m_new)
    p = jnp.exp(s - m_new)
    l_ref[...] = alpha * l_ref[...] + jnp.sum(p, axis=-1, keepdims=True)
    acc_ref[...] = alpha * acc_ref[...] + v_terms(p.astype(BF16))
    m_ref[...] = m_new


def _softmax_init(m_ref, l_ref, acc_ref):
    m_ref[...] = jnp.full(m_ref.shape, -jnp.inf, F32)
    l_ref[...] = jnp.zeros(l_ref.shape, F32)
    acc_ref[...] = jnp.zeros(acc_ref.shape, F32)


def _value_up_projection(acc_ref, l_ref, wuv_ref, y_ref, tq):
    for h in range(MLA_HEADS):
        rows = slice(h * tq, (h + 1) * tq)
        y = _dot_bf16(acc_ref[rows, :] / l_ref[rows, :], wuv_ref[h])
        y_ref[:, h * MLA_V_DIM:(h + 1) * MLA_V_DIM] = y.astype(y_ref.dtype)


PROMPT_ATTN_TILE = 256


def _mla_prompt_kernel(q_ref, k_ref, kt_ref, wuvt_ref, y_ref, m_ref, l_ref, acc_ref):
    tq = PROMPT_ATTN_TILE
    qi, ki = pl.program_id(1), pl.program_id(2)

    @pl.when(ki == 0)
    def _():
        _softmax_init(m_ref, l_ref, acc_ref)

    def step(on_diagonal):
        q = q_ref[...].reshape(MLA_HEADS * tq, MLA_QK)
        s = lax.dot_general(k_ref[...], q, NT_DIMS, preferred_element_type=F32) * MLA_SCALE
        if on_diagonal:
            k_pos = lax.broadcasted_iota(jnp.int32, s.shape, 0)
            q_pos = lax.broadcasted_iota(jnp.int32, s.shape, 1) & (tq - 1)
            s = jnp.where(k_pos <= q_pos, s, -jnp.inf)
        m_new = jnp.maximum(m_ref[...], jnp.max(s, axis=0, keepdims=True))
        alpha = jnp.exp(m_ref[...] - m_new)
        p = jnp.exp(s - m_new)
        l_ref[...] = alpha * l_ref[...] + jnp.sum(p, axis=0, keepdims=True)
        acc_ref[...] = alpha * acc_ref[...] + jnp.dot(kt_ref[...], p.astype(BF16),
                                                      preferred_element_type=F32)
        m_ref[...] = m_new

    pl.when(ki < qi)(lambda: step(False))
    pl.when(ki == qi)(lambda: step(True))

    @pl.when(ki == pl.num_programs(2) - 1)
    def _():
        o_t = (acc_ref[...] / l_ref[...]).astype(BF16)
        y_t = jnp.concatenate(
            [jnp.dot(wuvt_ref[h], o_t[:, h * tq:(h + 1) * tq], preferred_element_type=F32)
             for h in range(MLA_HEADS)], axis=0)
        y_ref[...] = y_t.T.astype(y_ref.dtype)


def mla_attend_prompt(q, k_full, k_lat_t, w_uv_t, batch):
    n = k_full.shape[0]
    t = PROMPT_ATTN_TILE
    n_t = n // batch // t
    rows = MLA_HEADS * t
    return pl.pallas_call(
        _mla_prompt_kernel,
        grid=(batch, n_t, n_t),
        in_specs=[pl.BlockSpec((MLA_HEADS, t, MLA_QK), lambda b, qi, ki: (0, b * n_t + qi, 0)),
                  pl.BlockSpec((t, MLA_QK), lambda b, qi, ki: (b * n_t + jnp.minimum(ki, qi), 0)),
                  pl.BlockSpec((MLA_KV_RANK, t), lambda b, qi, ki: (0, b * n_t + jnp.minimum(ki, qi))),
                  pl.BlockSpec(w_uv_t.shape, lambda b, qi, ki: (0, 0, 0))],
        out_specs=pl.BlockSpec((t, MLA_WIDTH), lambda b, qi, ki: (b * n_t + qi, 0)),
        out_shape=jax.ShapeDtypeStruct((n, MLA_WIDTH), BF16),
        scratch_shapes=[pltpu.VMEM((1, rows), F32), pltpu.VMEM((1, rows), F32),
                        pltpu.VMEM((MLA_KV_RANK, rows), F32)],
        compiler_params=_cparams(3),
        name="mla_attend_prompt",
    )(q, k_full, k_lat_t, w_uv_t)


PAGES_PER_STEP = 64


def _mla_sample_kernel(pt_ref, q_ref, *refs):
    del pt_ref
    pp = PAGES_PER_STEP
    lat_refs, rope_refs = refs[:pp], refs[pp:2 * pp]
    lat_new_ref, rope_new_ref, wuv_ref, y_ref, klat_ref, krope_t_ref, m_ref, l_ref, acc_ref = refs[2 * pp:]
    tq = q_ref.shape[1]
    j = pl.program_id(1)

    @pl.when(j == 0)
    def _():
        _softmax_init(m_ref, l_ref, acc_ref)

    q = q_ref[...].reshape(MLA_HEADS * tq, MLA_QK).astype(BF16)
    q_lat, q_rope = q[:, :MLA_KV_RANK], q[:, MLA_KV_RANK:MLA_KV_RANK + MLA_ROPE]

    for i in range(pp):
        span = slice(i * PAGE_SIZE, (i + 1) * PAGE_SIZE)
        klat_ref[span, :] = lat_refs[i][...].astype(BF16)
        krope_t_ref[:, span] = rope_refs[i][...].astype(BF16)
    s = (lax.dot_general(q_lat, klat_ref[...], NT_DIMS, preferred_element_type=F32)
         + jnp.dot(q_rope, krope_t_ref[...], preferred_element_type=F32)) * MLA_SCALE
    _softmax_update(s, lambda p: jnp.dot(p, klat_ref[...], preferred_element_type=F32), m_ref, l_ref, acc_ref)

    @pl.when(j == pl.num_programs(1) - 1)
    def _():
        pad = lambda x: jnp.concatenate([x, jnp.zeros((PAGE_SIZE - tq, x.shape[1]), F32)], axis=0).astype(BF16)
        lat_n = pad(lat_new_ref[...])
        s_n = (lax.dot_general(q_lat, lat_n, NT_DIMS, preferred_element_type=F32)
               + lax.dot_general(q_rope, pad(rope_new_ref[...]), NT_DIMS,
                                 preferred_element_type=F32)) * MLA_SCALE
        q_pos = lax.broadcasted_iota(jnp.int32, s_n.shape, 0) & (tq - 1)
        k_pos = lax.broadcasted_iota(jnp.int32, s_n.shape, 1)
        s_n = jnp.where(k_pos <= q_pos, s_n, -jnp.inf)
        _softmax_update(s_n, lambda p: jnp.dot(p, lat_n, preferred_element_type=F32), m_ref, l_ref, acc_ref)
        _value_up_projection(acc_ref, l_ref, wuv_ref, y_ref, tq)


def mla_attend_sample(q, lat_new, rope_new, cache_latent, cache_rope_t, page_table, layer, w_uv):
    bs, n_pages = page_table.shape
    tq = lat_new.shape[0] // bs
    pp = PAGES_PER_STEP
    page = lambda i: (lambda b, j, pt: (layer, pt[b, j * pp + i], 0, 0))
    in_specs = [pl.BlockSpec((MLA_HEADS, tq, MLA_QK), lambda b, j, pt: (0, b, 0))]
    in_specs += [pl.BlockSpec((None, None, PAGE_SIZE, MLA_KV_RANK), page(i)) for i in range(pp)]
    in_specs += [pl.BlockSpec((None, None, MLA_ROPE, PAGE_SIZE), page(i)) for i in range(pp)]
    in_specs += [pl.BlockSpec((tq, MLA_KV_RANK), lambda b, j, pt: (b, 0)),
                 pl.BlockSpec((tq, MLA_ROPE), lambda b, j, pt: (b, 0)),
                 pl.BlockSpec(w_uv.shape, lambda b, j, pt: (0, 0, 0))]
    rows = MLA_HEADS * tq
    return pl.pallas_call(
        _mla_sample_kernel,
        grid_spec=pltpu.PrefetchScalarGridSpec(
            num_scalar_prefetch=1,
            grid=(bs, n_pages // pp),
            in_specs=in_specs,
            out_specs=pl.BlockSpec((tq, MLA_WIDTH), lambda b, j, pt: (b, 0)),
            scratch_shapes=[pltpu.VMEM((pp * PAGE_SIZE, MLA_KV_RANK), BF16),
                            pltpu.VMEM((MLA_ROPE, pp * PAGE_SIZE), BF16),
                            pltpu.VMEM((rows, 1), F32), pltpu.VMEM((rows, 1), F32),
                            pltpu.VMEM((rows, MLA_KV_RANK), F32)]),
        out_shape=jax.ShapeDtypeStruct((bs * tq, MLA_WIDTH), F32),
        compiler_params=_cparams(2),
        name="mla_attend_sample",
    )(page_table, q, *([cache_latent] * pp), *([cache_rope_t] * pp), lat_new, rope_new, w_uv)


ROUTER_TILE = 256


def _top16_rows(s):
    iota = lax.broadcasted_iota(jnp.int32, s.shape, 0)
    rem = s
    vals = []
    for _ in range(PEER_TOPK):
        m = jnp.max(rem, axis=0, keepdims=True)
        idx = jnp.min(jnp.where(rem == m, iota, PEER_NKEYS), axis=0, keepdims=True)
        rem = jnp.where(iota == idx, -jnp.inf, rem)
        vals.append(m)
    return rem != s, vals


def _staircase_candidates(sv0, sv1):
    pairs = [(a, b) for a in range(PEER_TOPK) for b in range(PEER_TOPK) if (a + 1) * (b + 1) <= PEER_TOPK]
    rows = [sv0[a] + sv1[b] for a, b in pairs]
    rows += [jnp.full_like(rows[0], -jnp.inf)] * (-len(rows) % 8)
    return jnp.concatenate(rows, axis=0)


def _router_kernel(q_ref, keys_ref, s0_ref, s1_ref, e0_ref, e1_ref, thr_ref):
    half = PEER_DKEY // 2

    def per_head(h, carry):
        def scores(side):
            col = pl.multiple_of(h * PEER_DKEY + side * half, half)
            qh = q_ref[:, pl.ds(col, half)].astype(BF16)
            return lax.dot_general(keys_ref[h, side], qh, NT_DIMS, preferred_element_type=F32)

        s0, s1 = scores(0), scores(1)
        mask0, sv0 = _top16_rows(s0)
        mask1, sv1 = _top16_rows(s1)
        cand = _staircase_candidates(sv0, sv1)
        cmax = sv0[0] + sv1[0]
        rem = cand
        cum = jnp.zeros_like(cmax)
        thr = cmax
        for _ in range(PEER_TOPK):
            m = jnp.max(rem, axis=0, keepdims=True)
            eq = rem == m
            thr = jnp.where(cum < PEER_TOPK, m, thr)
            cum = cum + jnp.sum(eq.astype(F32), axis=0, keepdims=True)
            rem = jnp.where(eq, -jnp.inf, rem)
        z = jnp.sum(jnp.where(cand >= thr, jnp.exp(cand - cmax), 0.0), axis=0, keepdims=True)
        s0_ref[h] = s0
        s1_ref[h] = s1
        e0_ref[h] = jnp.where(mask0, jnp.exp(s0 - sv0[0]), 0.0)
        e1_ref[h] = jnp.where(mask1, jnp.exp(s1 - sv1[0]), 0.0) / z
        thr_ref[pl.ds(h, 1), :] = thr
        return carry

    lax.fori_loop(0, PEER_HEADS, per_head, 0)


def peer_router(q, keys_bf16):
    n = q.shape[0]
    t = ROUTER_TILE
    big = jax.ShapeDtypeStruct((PEER_HEADS, PEER_NKEYS, n), F32)
    big_spec = pl.BlockSpec((PEER_HEADS, PEER_NKEYS, t), lambda i: (0, 0, i))
    return pl.pallas_call(
        _router_kernel,
        grid=(n // t,),
        in_specs=[pl.BlockSpec((t, PEER_HEADS * PEER_DKEY), lambda i: (i, 0)),
                  pl.BlockSpec(keys_bf16.shape, lambda i: (0, 0, 0, 0))],
        out_specs=[big_spec, big_spec, big_spec, big_spec,
                   pl.BlockSpec((PEER_HEADS, t), lambda i: (0, i))],
        out_shape=[big, big, big, big, jax.ShapeDtypeStruct((PEER_HEADS, n), F32)],
        compiler_params=_cparams(1),
        name="peer_router",
    )(q, keys_bf16)


PEER_TOKEN_TILE = 512
PEER_EXPERT_TILE = 512


def _gelu_exact(a):
    return 0.5 * a * (1.0 + lax.erf(a * (2.0 ** -0.5)))


def _peer_dense_kernel(hn_ref, u_ref, v_ref, s0_ref, s1_ref, e0_ref, e1_ref, thr_ref, o_ref, coef_ref):
    e = pl.program_id(1)
    rows_per_tile = PEER_EXPERT_TILE // PEER_NKEYS
    last_tile = pl.num_programs(1) - 2

    @pl.when(e == 0)
    def _():
        o_ref[...] = jnp.zeros_like(o_ref)
        coef_ref[1] = jnp.zeros(coef_ref.shape[1:], BF16)

    o_ref[...] += lax.dot_general(coef_ref[(e + 1) % 2], v_ref[...], TN_DIMS,
                                  preferred_element_type=F32)

    act_t = lax.dot_general(u_ref[...], hn_ref[...], NT_DIMS, preferred_element_type=F32)
    coef = []
    for ib in range(rows_per_tile):
        i = jnp.minimum(e, last_tile) * rows_per_tile + ib
        gate = jnp.zeros((PEER_NKEYS, act_t.shape[1]), F32)
        for h in range(PEER_HEADS):
            picked = (s0_ref[h, pl.ds(i, 1), :] + s1_ref[h]) >= thr_ref[pl.ds(h, 1), :]
            gate = gate + jnp.where(picked, e0_ref[h, pl.ds(i, 1), :] * e1_ref[h], 0.0)
        a = act_t[ib * PEER_NKEYS:(ib + 1) * PEER_NKEYS]
        coef.append((gate * _gelu_exact(a)).astype(BF16))
    coef_ref[e % 2] = jnp.concatenate(coef, axis=0)


def peer_dense(hn, u_all, v_all, layer, s0, s1, e0, e1, thr):
    n, d = hn.shape
    tm, te = PEER_TOKEN_TILE, PEER_EXPERT_TILE
    n_e = PEER_EXPERTS // te
    once = pl.Buffered(1)
    big_spec = pl.BlockSpec((PEER_HEADS, PEER_NKEYS, tm), lambda i, e: (0, 0, i), pipeline_mode=once)
    return pl.pallas_call(
        _peer_dense_kernel,
        grid=(n // tm, n_e + 1),
        in_specs=[pl.BlockSpec((tm, d), lambda i, e: (i, 0), pipeline_mode=once),
                  pl.BlockSpec((None, te, d), lambda i, e: (layer, jnp.minimum(e, n_e - 1), 0)),
                  pl.BlockSpec((None, te, d), lambda i, e: (layer, jnp.maximum(e - 1, 0), 0)),
                  big_spec, big_spec, big_spec, big_spec,
                  pl.BlockSpec((PEER_HEADS, tm), lambda i, e: (0, i), pipeline_mode=once)],
        out_specs=pl.BlockSpec((tm, d), lambda i, e: (i, 0)),
        out_shape=jax.ShapeDtypeStruct((n, d), F32),
        scratch_shapes=[pltpu.VMEM((2, te, tm), BF16)],
        compiler_params=_cparams(2),
        name="peer_dense",
    )(hn, u_all, v_all, s0, s1, e0, e1, thr)


def _gated_residual_kernel(x_ref, gate_ref, y_ref, o_ref):
    o_ref[...] = x_ref[...] + gate_ref[...] * y_ref[...].reshape(x_ref.shape)


def gated_residual(x3, gate, y):
    gdim, r, d = x3.shape
    tg, rb = _token_tiling(x3)
    n_r = r // rb
    return pl.pallas_call(
        _gated_residual_kernel,
        grid=(gdim // tg, n_r),
        in_specs=[pl.BlockSpec((tg, rb, d), lambda gi, ri: (gi, ri, 0)),
                  pl.BlockSpec((tg, 1, d), lambda gi, ri: (gi, 0, 0)),
                  pl.BlockSpec((tg * rb, d), lambda gi, ri: (gi * n_r + ri, 0))],
        out_specs=pl.BlockSpec((tg, rb, d), lambda gi, ri: (gi, ri, 0)),
        out_shape=jax.ShapeDtypeStruct(x3.shape, F32),
        compiler_params=_cparams(2),
        name="gated_residual",
    )(x3, gate, y)


def _rmsnorm_kernel(x_ref, g_ref, o_ref):
    x = x_ref[...]
    o_ref[...] = x * lax.rsqrt(jnp.mean(x * x, axis=-1, keepdims=True) + EPS) * g_ref[...]


def rmsnorm_rows(x3, g):
    gdim, r, d = x3.shape
    tg, rb = _token_tiling(x3)
    return pl.pallas_call(
        _rmsnorm_kernel,
        grid=(gdim // tg, r // rb),
        in_specs=[pl.BlockSpec((tg, rb, d), lambda gi, ri: (gi, ri, 0)),
                  pl.BlockSpec((1, 1, d), lambda gi, ri: (0, 0, 0))],
        out_specs=pl.BlockSpec((tg, rb, d), lambda gi, ri: (gi, ri, 0)),
        out_shape=jax.ShapeDtypeStruct(x3.shape, F32),
        compiler_params=_cparams(2),
        name="final_rmsnorm",
    )(x3, g.reshape(1, 1, d))


def _decoder_layer(x3, mod, pool_hist, conv_hist, ssm_state, caches, start, p):
    b, l, d = x3.shape
    m = lambda i: mod[:, i, None, :]
    is_prompt = caches is None
    act_dtype = BF16 if is_prompt else F32
    u = norm_mod_matmul(x3, p['g_mix'], 1.0 + m(1), m(0), p['w_in'], tn=640, name="in_proj")
    y_pool, new_pool = pool_mixer(u, pool_hist, start, p['w_pool'], p['pool_scale'], act_dtype)
    y_ssd, new_conv, new_ssm = ssd_mixer(u, conv_hist, ssm_state, p['conv_w'], p['conv_b'], p['dt_bias'],
                                         p['a_log'], p['d_skip'], p['ssd_norm'], act_dtype,
                                         layer=None if is_prompt else p['layer'])
    q, k_full, k_lat_t, lat, krope = mla_prep(u, l, start, p['g_q'], p['g_kv'], p['w_uq'], p['w_uk'], act_dtype)
    if is_prompt:
        y_mla = mla_attend_prompt(q, k_full, k_lat_t, p['w_uv_t'], b)
    else:
        y_mla = mla_attend_sample(q, lat, krope, *caches, p['w_uv'])
    x3 = proj_residual(y_pool, y_ssd, y_mla, *p['w_out'], x3, m(2))
    q, hn = norm_mod_matmul(x3, p['g_ffn'], 1.0 + m(4), m(3), p['peer_wq'], tn=512, emit_hn=True, name="peer_query")
    s0, s1, e0, e1, thr = peer_router(q, p['peer_keys'])
    y = peer_dense(hn, p['peer_u'], p['peer_v'], p['layer'], s0, s1, e0, e1, thr)
    x3 = gated_residual(x3, m(5), y)
    return (x3, new_pool, new_conv, new_ssm,
            lat.reshape(b, l, MLA_KV_RANK), krope.reshape(b, l, MLA_ROPE))


def _pack_w_in(w):
    d = w.shape[0]
    sizes = (POOL_WIDTH, SSD_WIDTH, SSD_CONV_DIM, SSD_HEADS, MLA_Q_RANK, MLA_KV_RANK, MLA_ROPE)
    offs = [0]
    for s in sizes:
        offs.append(offs[-1] + s)
    pool, z, xbc, dt, cq, ckv, kr = (w[:, offs[i]:offs[i + 1]] for i in range(len(sizes)))
    zeros = lambda k: jnp.zeros((d, k), w.dtype)
    packed = jnp.concatenate([z, xbc, pool, cq, ckv, kr, zeros(LANES - MLA_ROPE),
                              dt, zeros(LANES - SSD_HEADS)], axis=1)
    assert packed.shape[1] == D_IN_PACKED
    return packed.astype(BF16)


def _pack_w_uq(w):
    r = w.shape[0]
    padded = jnp.concatenate([w, jnp.zeros((r, MLA_HEADS, MLA_QHEAD - w.shape[2]), w.dtype)], axis=-1)
    return padded.reshape(r, MLA_HEADS * MLA_QHEAD).astype(BF16)


def kernel(x_prompt, x_sample, cache_latent, cache_rope, state_ssm, state_conv, state_pool, page_table, c_prompt, c_sample, w_ada, b_ada, ada_table, g_mix, w_in, w_pool, pool_scale, conv_w, conv_b, dt_bias, a_log, d_skip, ssd_norm, g_q, w_uq, g_kv, w_uk, w_uv, w_out, g_ffn, peer_wq, peer_keys, peer_u, peer_v, g_final):
    depth = w_in.shape[0]
    bp = x_prompt.shape[0]
    bs = x_sample.shape[0]
    past_len = page_table.shape[1] * PAGE_SIZE

    n_c = bp + bs
    c_all = jnp.concatenate([c_prompt, c_sample, jnp.zeros((-n_c % 8, D_MODEL), F32)], axis=0)
    mod_all = adaln_base(c_all, w_ada, b_ada)
    mod_p = mod_all[:bp].reshape(bp, N_MOD, D_MODEL)
    mod_s = mod_all[bp:n_c].reshape(bs, N_MOD, D_MODEL)

    zero_pool = jnp.zeros((bp, POOL_HIST, POOL_WIDTH), F32)
    zero_conv = jnp.zeros((bp, SSD_CONV - 1, SSD_CONV_DIM), F32)
    zero_ssm = jnp.zeros((bp, SSD_HEADS, SSD_HEAD_DIM, SSD_STATE), F32)

    peer_u_bf16, peer_v_bf16 = peer_u.astype(BF16), peer_v.astype(BF16)
    cache_rope_t = jnp.swapaxes(cache_rope, 2, 3)

    hp, hs = x_prompt, x_sample
    outs = [[] for _ in range(10)]
    for l in range(depth):
        w_out_l = w_out[l].astype(BF16)
        p = {'g_mix': g_mix[l], 'w_in': _pack_w_in(w_in[l]), 'w_pool': w_pool[l].astype(BF16),
             'pool_scale': pool_scale[l], 'conv_w': conv_w[l], 'conv_b': conv_b[l], 'dt_bias': dt_bias[l],
             'a_log': a_log[l], 'd_skip': d_skip[l], 'ssd_norm': ssd_norm[l], 'g_q': g_q[l],
             'w_uq': _pack_w_uq(w_uq[l]), 'g_kv': g_kv[l],
             'w_uk': jnp.transpose(w_uk[l], (1, 2, 0)).astype(BF16),
             'w_uv': jnp.transpose(w_uv[l], (1, 0, 2)).astype(BF16),
             'w_uv_t': jnp.transpose(w_uv[l], (1, 2, 0)).astype(BF16),
             'w_out': (w_out_l[:POOL_WIDTH], w_out_l[POOL_WIDTH:POOL_WIDTH + SSD_WIDTH],
                       w_out_l[POOL_WIDTH + SSD_WIDTH:]),
             'g_ffn': g_ffn[l], 'peer_wq': peer_wq[l].astype(BF16), 'peer_keys': peer_keys[l].astype(BF16),
             'peer_u': peer_u_bf16, 'peer_v': peer_v_bf16, 'layer': l}
        hp, np_pool, np_conv, np_ssm, np_lat, np_rope = _decoder_layer(
            hp, mod_p + ada_table[l], zero_pool, zero_conv, zero_ssm, None, 0, p)
        hs, ns_pool, ns_conv, ns_ssm, ns_lat, ns_rope = _decoder_layer(
            hs, mod_s + ada_table[l], state_pool[l], state_conv, state_ssm,
            (cache_latent, cache_rope_t, page_table, l), past_len, p)
        for lst, val in zip(outs, (np_lat, np_rope, ns_lat, ns_rope, np_ssm, ns_ssm, np_conv, ns_conv,
                                   np_pool, ns_pool)):
            lst.append(val)
    y_prompt = rmsnorm_rows(hp, g_final)
    y_sample = rmsnorm_rows(hs, g_final)
    return (y_prompt, y_sample) + tuple(jnp.stack(o) for o in outs)
```
